```python
import math
import jax, jax.numpy as jnp
from jax import lax
import numpy as np

D_MODEL = 1024
BATCH = 8
SEQ = 4096
DEPTH = 1
DEC_BATCH = 128
DEC_SEQ = 4
PAST_LEN = 8192
PAGE_SIZE = 128

HEAD_DIM = 64
NSA_HEADS = 8
NSA_KV = 2
NSA_REP = NSA_HEADS // NSA_KV
CMP_LEN = 32
CMP_STRIDE = 16
CMP_HIDDEN = 128
SEL_BLOCK = 64
N_SEL = 16
WINDOW = 512
FORCED_SCORE = 1e4
DSA_HEADS = 8
DSA_KV = 2
DSA_REP = DSA_HEADS // DSA_KV
IDX_HEADS = 4
IDX_DIM = 64
DSA_TOPK_MAX = 256
MEM_LEN = 256
X_HEADS = 4
X_HEAD_DIM = 64
N_EXPERTS = 32
TOP_K = 4
D_FF = 1024
SWIGLU_LIMIT = 7.0
SWIGLU_ALPHA = 1.702
MOE_BLOCK = 128
N_BUCKETS = 32
REL_MAX_DIST = 128
Q_BLOCK = 128
RMS_EPS = 1e-6
NEG_INF = -1e30
N_REL_HEADS = NSA_HEADS + DSA_HEADS
SPLITS = (("nsa_q", NSA_HEADS * HEAD_DIM), ("nsa_kv", 6 * NSA_KV * HEAD_DIM), ("nsa_gate", 3 * NSA_HEADS),
          ("dsa_q", DSA_HEADS * HEAD_DIM), ("dsa_kv", 2 * DSA_KV * HEAD_DIM), ("idx_q", IDX_HEADS * IDX_DIM),
          ("idx_k", IDX_DIM), ("idx_w", IDX_HEADS), ("merge_gate", 2 * D_MODEL))
D_IN = sum(n for _, n in SPLITS)

kernel_name = "nsa_dsa_gated_hybrid_moe_step"


def rmsnorm(x, g):
    xf = x.astype(jnp.float32)
    y = xf * lax.rsqrt(jnp.mean(xf * xf, axis=-1, keepdims=True) + RMS_EPS)
    return (y * g.astype(jnp.float32)).astype(x.dtype)


def split_proj(h):
    out, off = {}, 0
    for name, n in SPLITS:
        out[name] = h[..., off:off + n]
        off += n
    return out


def t5_bucket(dist):
    n = jnp.maximum(dist, 0)
    max_exact = N_BUCKETS // 2
    nf = jnp.maximum(n, 1).astype(jnp.float32)
    large = max_exact + (jnp.log(nf / max_exact) / math.log(REL_MAX_DIST / max_exact)
                         * (N_BUCKETS - max_exact)).astype(jnp.int32)
    return jnp.where(n < max_exact, n, jnp.minimum(large, N_BUCKETS - 1))


def masked_softmax(s, mask):
    p = jax.nn.softmax(jnp.where(mask, s, NEG_INF), axis=-1)
    return jnp.where(mask, p, 0.0)


def cmp_to_sel_matrix(nc, ns):
    c0 = np.arange(nc) * CMP_STRIDE
    s0 = np.arange(ns) * SEL_BLOCK
    ov = np.minimum(c0[:, None] + CMP_LEN, s0[None, :] + SEL_BLOCK) - np.maximum(c0[:, None], s0[None, :])
    return jnp.asarray(np.clip(ov, 0, None) / CMP_LEN, dtype=jnp.float32)


def compress(rows, pe, w1, w2):
    B, L, G, D = rows.shape
    r = CMP_LEN // CMP_STRIDE
    nch = L // CMP_STRIDE
    nc = nch - r + 1
    ch = rows[:, :nch * CMP_STRIDE].reshape(B, nch, CMP_STRIDE, G, D)
    w1c = w1.reshape(r, CMP_STRIDE, D, CMP_HIDDEN)
    hid = jnp.einsum('ld,ldh->h', pe, w1)
    for i in range(r):
        hid = hid + jnp.einsum('bnsgd,sdh->bngh', ch[:, i:i + nc], w1c[i])
    return jax.nn.silu(hid) @ w2


def gather_pages(pool, page_table):
    nb, npg = page_table.shape
    return pool[page_table].reshape((nb, npg * pool.shape[1]) + pool.shape[2:])


def paged_rows(pool, page_table, new_rows, pos, past_len, gi=None):
    nb = pos.shape[0]
    bi = jnp.arange(nb).reshape((nb,) + (1,) * (pos.ndim - 1))
    pp = jnp.minimum(pos, past_len - 1)
    phys = page_table[bi, pp // pool.shape[1]]
    off = pp % pool.shape[1]
    npos = jnp.clip(pos - past_len, 0, new_rows.shape[1] - 1)
    if gi is None:
        past, new = pool[phys, off], new_rows[bi, npos]
    else:
        past, new = pool[phys, off, :, gi], new_rows[bi, npos, :, gi]
    is_past = (pos < past_len).reshape(pos.shape + (1,) * (past.ndim - pos.ndim))
    return jnp.where(is_past, past, new)


def nsa_attend(q, q_pos, gates, kc, vc, c_end, gather_sel, n_keys, win_rows, w_pos, rel_tab):
    B, T = q.shape[:2]
    scale = HEAD_DIM ** -0.5
    qg = q.reshape(B, T, NSA_KV, NSA_REP, HEAD_DIM)
    d_c = q_pos[:, None] - c_end[None, :]
    s = jnp.einsum('btgrd,bcgd->btgrc', qg, kc).astype(jnp.float32) * scale
    s = s + jnp.moveaxis(rel_tab[t5_bucket(d_c)], 1, -1)
    p_c = masked_softmax(s, (d_c >= 0)[:, None, None, :])
    o_c = jnp.einsum('btgrc,bcgd->btgrd', p_c.astype(vc.dtype), vc)
    nc = kc.shape[1]
    ns = -(-n_keys // SEL_BLOCK)
    blk = jnp.einsum('btgrc,cj->btgj', p_c, cmp_to_sel_matrix(nc, ns))
    j = jnp.arange(ns)
    cur = (q_pos // SEL_BLOCK)[:, None]
    visible = ((j * SEL_BLOCK)[None, :] <= q_pos[:, None])[:, None, :]
    forced = ((j[None, :] == 0) | (j[None, :] == cur) | (j[None, :] == cur - 1))[:, None, :]
    score = jnp.where(visible, jnp.where(forced, FORCED_SCORE, blk), -jnp.inf)
    _, idx = lax.top_k(score, min(N_SEL, ns))
    tok = (idx[..., None] * SEL_BLOCK + jnp.arange(SEL_BLOCK)).reshape(B, T, NSA_KV, -1)
    rows = gather_sel(jnp.minimum(tok, n_keys - 1))
    d_s = q_pos[None, :, None, None] - tok
    s = jnp.einsum('btgrd,btgkd->btgrk', qg, rows[..., 0, :]).astype(jnp.float32) * scale
    bias = rel_tab[t5_bucket(d_s), jnp.arange(NSA_KV)[:, None]]
    p_s = masked_softmax(s + jnp.moveaxis(bias, -1, 3), (d_s >= 0)[:, :, :, None, :])
    o_s = jnp.einsum('btgrk,btgkd->btgrd', p_s.astype(rows.dtype), rows[..., 1, :])
    d_w = q_pos[:, None] - w_pos[None, :]
    s = jnp.einsum('btgrd,bwgd->btgrw', qg, win_rows[:, :, 0]).astype(jnp.float32) * scale
    s = s + jnp.moveaxis(rel_tab[t5_bucket(d_w)], 1, -1)
    ok_w = (d_w >= 0) & (d_w <= WINDOW) & (w_pos >= 0)[None, :]
    p_w = masked_softmax(s, ok_w[:, None, None, :])
    o_w = jnp.einsum('btgrw,bwgd->btgrd', p_w.astype(win_rows.dtype), win_rows[:, :, 1])
    g = gates.reshape(B, T, NSA_KV, NSA_REP, 3).astype(o_c.dtype)
    o = o_c * g[..., 0:1] + o_s * g[..., 1:2] + o_w * g[..., 2:3]
    return o.reshape(B, T, NSA_HEADS * HEAD_DIM)


def dsa_attend(q, q_pos, qi, wi, ki, gather_kv, rel_tab):
    B, T = q.shape[:2]
    n_keys = ki.shape[1]
    s_i = jnp.einsum('bthe,ble->bthl', qi, ki).astype(jnp.float32) * IDX_DIM ** -0.5
    score = jnp.einsum('bthl,bth->btl', jax.nn.relu(s_i), wi.astype(jnp.float32))
    admissible = jnp.arange(n_keys)[None, :] <= q_pos[:, None]
    score = jnp.where(admissible[None], score, -jnp.inf)
    _, idx = lax.top_k(score, min(DSA_TOPK_MAX, n_keys // 4))
    rows = gather_kv(idx)
    qg = q.reshape(B, T, DSA_KV, DSA_REP, HEAD_DIM)
    d = q_pos[None, :, None] - idx
    s = jnp.einsum('btgrd,btkgd->btgrk', qg, rows[:, :, :, 0]).astype(jnp.float32) * HEAD_DIM ** -0.5
    s = s + jnp.moveaxis(rel_tab[t5_bucket(d)], 2, -1)
    p = masked_softmax(s, (d >= 0)[:, :, None, None, :])
    o = jnp.einsum('btgrk,btkgd->btgrd', p.astype(rows.dtype), rows[:, :, :, 1])
    return o.reshape(B, T, DSA_HEADS * HEAD_DIM)


def mem_attend(h, mem_kv, w_q, w_o):
    B, T = h.shape[:2]
    q = (h @ w_q).reshape(B, T, X_HEADS, X_HEAD_DIM)
    s = jnp.einsum('bthd,bmhd->bthm', q, mem_kv[:, :, 0]).astype(jnp.float32) * X_HEAD_DIM ** -0.5
    p = jax.nn.softmax(s, axis=-1)
    o = jnp.einsum('bthm,bmhd->bthd', p.astype(mem_kv.dtype), mem_kv[:, :, 1])
    return o.reshape(B, T, X_HEADS * X_HEAD_DIM) @ w_o


def moe(h, w_router, b_router, w_gu, b_gu, w_down, b_down):
    shp = h.shape
    xt = h.reshape(-1, shp[-1])
    n = xt.shape[0]
    logits = (xt @ w_router).astype(jnp.float32) + b_router.astype(jnp.float32)
    top_v, top_e = lax.top_k(logits, TOP_K)
    gate = jax.nn.softmax(top_v, axis=-1)
    a = n * TOP_K
    e_flat = top_e.reshape(-1)
    t_flat = jnp.repeat(jnp.arange(n), TOP_K)
    order = jnp.argsort(e_flat)
    es, ts, ws = e_flat[order], t_flat[order], gate.reshape(-1)[order]
    counts = jnp.bincount(e_flat, length=N_EXPERTS)
    padded = (counts + MOE_BLOCK - 1) // MOE_BLOCK * MOE_BLOCK
    pad_end = jnp.cumsum(padded)
    pad_start = pad_end - padded
    raw_start = jnp.cumsum(counts) - counts
    dest = pad_start[es] + jnp.arange(a) - raw_start[es]
    n_blocks = -(-a // MOE_BLOCK) + N_EXPERTS
    slot_tok = jnp.zeros((n_blocks * MOE_BLOCK,), jnp.int32).at[dest].set(ts)
    slot_w = jnp.zeros((n_blocks * MOE_BLOCK,), jnp.float32).at[dest].set(ws)
    blk_e = jnp.minimum(jnp.searchsorted(pad_end, jnp.arange(n_blocks) * MOE_BLOCK, side='right'), N_EXPERTS - 1)

    def run_block(args):
        tok_b, w_b, e = args
        xb = xt[tok_b]
        gu = xb @ w_gu[e] + b_gu[e]
        g = jnp.minimum(gu[:, :D_FF], SWIGLU_LIMIT)
        u = jnp.clip(gu[:, D_FF:], -SWIGLU_LIMIT, SWIGLU_LIMIT)
        hb = (u + 1) * (g * jax.nn.sigmoid(SWIGLU_ALPHA * g))
        return (hb @ w_down[e] + b_down[e]) * w_b[:, None].astype(xb.dtype)

    yb = lax.map(run_block, (slot_tok.reshape(n_blocks, MOE_BLOCK), slot_w.reshape(n_blocks, MOE_BLOCK), blk_e))
    y = jnp.zeros_like(xt).at[slot_tok].add(yb.reshape(-1, shp[-1]))
    return y.reshape(shp)


def layer_tail(x, o_n, o_d, merge_gate, mem_kv, w_nsa_out, w_dsa_out, w_o, ln_mem, w_mem_q, w_mem_o,
               ln_ffn, w_router, b_router, w_gu, b_gu, w_down, b_down, ln_f):
    g = jax.nn.sigmoid(merge_gate)
    x = x + (g[..., :D_MODEL] * (o_n @ w_nsa_out) + g[..., D_MODEL:] * (o_d @ w_dsa_out)) @ w_o
    x = x + mem_attend(rmsnorm(x, ln_mem), mem_kv, w_mem_q, w_mem_o)
    x = x + moe(rmsnorm(x, ln_ffn), w_router, b_router, w_gu, b_gu, w_down, b_down)
    return rmsnorm(x, ln_f)


def setup_inputs(seed: int = 0) -> dict:
    key = jax.random.key(seed)
    ks = iter(jax.random.split(key, 48))

    def nrm(shape, scale):
        return jax.random.normal(next(ks), shape, jnp.float32) * scale

    n_pages = PAST_LEN // PAGE_SIZE
    n_used = DEC_BATCH * n_pages
    n_phys = n_used + max(1, n_used // 4)
    wb = min(WINDOW, PAST_LEN)
    inp = {}
    inp["x_prompt"] = nrm((BATCH, SEQ, D_MODEL), 1.0)
    inp["x_sample"] = nrm((DEC_BATCH, DEC_SEQ, D_MODEL), 1.0)
    inp["cache_nsa_cmp_kv"] = nrm((n_phys, PAGE_SIZE, 2, NSA_KV, HEAD_DIM), 1.0)
    inp["cache_nsa_sel_kv"] = nrm((n_phys, PAGE_SIZE, 2, NSA_KV, HEAD_DIM), 1.0)
    inp["cache_nsa_win_kv"] = nrm((DEC_BATCH, wb, 2, NSA_KV, HEAD_DIM), 1.0)
    inp["cache_dsa_kv"] = nrm((n_phys, PAGE_SIZE, 2, DSA_KV, HEAD_DIM), 1.0)
    inp["cache_dsa_idx_k"] = nrm((n_phys, PAGE_SIZE, IDX_DIM), 1.0)
    inp["cache_mem_kv"] = nrm((DEC_BATCH, MEM_LEN, 2, X_HEADS, X_HEAD_DIM), 1.0)
    inp["page_table"] = jax.random.permutation(next(ks), n_phys)[:n_used].reshape(DEC_BATCH, n_pages).astype(jnp.int32)
    inp["mem_prompt"] = nrm((BATCH, MEM_LEN, D_MODEL), 1.0)
    inp["ln_mix"] = 1.0 + nrm((D_MODEL,), 0.02)
    inp["w_in"] = nrm((D_MODEL, D_IN), D_MODEL ** -0.5)
    inp["cmp_k_pe"] = nrm((CMP_LEN, HEAD_DIM), 0.1)
    inp["cmp_k_w1"] = nrm((CMP_LEN, HEAD_DIM, CMP_HIDDEN), (CMP_LEN * HEAD_DIM) ** -0.5)
    inp["cmp_k_w2"] = nrm((CMP_HIDDEN, HEAD_DIM), CMP_HIDDEN ** -0.5)
    inp["cmp_v_pe"] = nrm((CMP_LEN, HEAD_DIM), 0.1)
    inp["cmp_v_w1"] = nrm((CMP_LEN, HEAD_DIM, CMP_HIDDEN), (CMP_LEN * HEAD_DIM) ** -0.5)
    inp["cmp_v_w2"] = nrm((CMP_HIDDEN, HEAD_DIM), CMP_HIDDEN ** -0.5)
    inp["rel_bias"] = nrm((N_BUCKETS, N_REL_HEADS), 0.3)
    inp["w_nsa_out"] = nrm((NSA_HEADS * HEAD_DIM, D_MODEL), (NSA_HEADS * HEAD_DIM) ** -0.5)
    inp["w_dsa_out"] = nrm((DSA_HEADS * HEAD_DIM, D_MODEL), (DSA_HEADS * HEAD_DIM) ** -0.5)
    inp["w_o"] = nrm((D_MODEL, D_MODEL), D_MODEL ** -0.5)
    inp["ln_mem"] = 1.0 + nrm((D_MODEL,), 0.02)
    inp["w_mem_q"] = nrm((D_MODEL, X_HEADS * X_HEAD_DIM), D_MODEL ** -0.5)
    inp["w_mem_kv"] = nrm((D_MODEL, 2 * X_HEADS * X_HEAD_DIM), D_MODEL ** -0.5)
    inp["w_mem_o"] = nrm((X_HEADS * X_HEAD_DIM, D_MODEL), (X_HEADS * X_HEAD_DIM) ** -0.5)
    inp["ln_ffn"] = 1.0 + nrm((D_MODEL,), 0.02)
    inp["w_router"] = nrm((D_MODEL, N_EXPERTS), D_MODEL ** -0.5)
    inp["b_router"] = nrm((N_EXPERTS,), 0.01)
    inp["w_gu"] = nrm((N_EXPERTS, D_MODEL, 2 * D_FF), D_MODEL ** -0.5)
    inp["b_gu"] = nrm((N_EXPERTS, 2 * D_FF), 0.01)
    inp["w_down"] = nrm((N_EXPERTS, D_FF, D_MODEL), D_FF ** -0.5)
    inp["b_down"] = nrm((N_EXPERTS, D_MODEL), 0.01)
    inp["ln_f"] = 1.0 + nrm((D_MODEL,), 0.02)
    return inp


def reference(x_prompt, x_sample, cache_nsa_cmp_kv, cache_nsa_sel_kv, cache_nsa_win_kv, cache_dsa_kv,
              cache_dsa_idx_k, cache_mem_kv, page_table, mem_prompt, ln_mix, w_in, cmp_k_pe, cmp_k_w1, cmp_k_w2,
              cmp_v_pe, cmp_v_w1, cmp_v_w2, rel_bias, w_nsa_out, w_dsa_out, w_o, ln_mem, w_mem_q, w_mem_kv,
              w_mem_o, ln_ffn, w_router, b_router, w_gu, b_gu, w_down, b_down, ln_f):
    rel_nsa = rel_bias[:, :NSA_HEADS].reshape(N_BUCKETS, NSA_KV, NSA_REP)
    rel_dsa = rel_bias[:, NSA_HEADS:].reshape(N_BUCKETS, DSA_KV, DSA_REP)
    tail_w = (w_nsa_out, w_dsa_out, w_o, ln_mem, w_mem_q, w_mem_o, ln_ffn, w_router, b_router,
              w_gu, b_gu, w_down, b_down, ln_f)

    B, L = x_prompt.shape[:2]
    pp = split_proj(rmsnorm(x_prompt, ln_mix) @ w_in)
    q_n = pp["nsa_q"].reshape(B, L, NSA_HEADS, HEAD_DIM)
    g_n = jax.nn.sigmoid(pp["nsa_gate"]).reshape(B, L, NSA_HEADS, 3)
    kv6 = pp["nsa_kv"].reshape(B, L, 6, NSA_KV, HEAD_DIM)
    kc = compress(kv6[:, :, 0], cmp_k_pe, cmp_k_w1, cmp_k_w2)
    vc = compress(kv6[:, :, 1], cmp_v_pe, cmp_v_w1, cmp_v_w2)
    c_end = jnp.arange(kc.shape[1]) * CMP_STRIDE + CMP_LEN - 1
    sel_rows = kv6[:, :, 2:4]
    win_pad = jnp.pad(kv6[:, :, 4:6], ((0, 0), (WINDOW, 0), (0, 0), (0, 0), (0, 0)))
    q_d = pp["dsa_q"].reshape(B, L, DSA_HEADS, HEAD_DIM)
    dsa_rows = pp["dsa_kv"].reshape(B, L, 2, DSA_KV, HEAD_DIM)
    qi = pp["idx_q"].reshape(B, L, IDX_HEADS, IDX_DIM)
    ki = pp["idx_k"]
    wi = pp["idx_w"]
    bi4 = jnp.arange(B)[:, None, None, None]
    gi4 = jnp.arange(NSA_KV)[None, None, :, None]
    bi3 = jnp.arange(B)[:, None, None]

    def prompt_block(start):
        sl = lambda a: lax.dynamic_slice_in_dim(a, start, Q_BLOCK, axis=1)
        q_pos = start + jnp.arange(Q_BLOCK)
        win_rows = lax.dynamic_slice_in_dim(win_pad, start, Q_BLOCK + WINDOW, axis=1)
        w_pos = start - WINDOW + jnp.arange(Q_BLOCK + WINDOW)
        o_n = nsa_attend(sl(q_n), q_pos, sl(g_n), kc, vc, c_end, lambda tok: sel_rows[bi4, tok, :, gi4],
                         L, win_rows, w_pos, rel_nsa)
        o_d = dsa_attend(sl(q_d), q_pos, sl(qi), sl(wi), ki, lambda idx: dsa_rows[bi3, idx], rel_dsa)
        return o_n, o_d

    o_n, o_d = lax.map(prompt_block, jnp.arange(L // Q_BLOCK) * Q_BLOCK)
    o_n = jnp.moveaxis(o_n, 0, 1).reshape(B, L, -1)
    o_d = jnp.moveaxis(o_d, 0, 1).reshape(B, L, -1)
    mem_kv_p = (mem_prompt @ w_mem_kv).reshape(B, MEM_LEN, 2, X_HEADS, X_HEAD_DIM)
    y_prompt = layer_tail(x_prompt, o_n, o_d, pp["merge_gate"], mem_kv_p, *tail_w)

    DB, S = x_sample.shape[:2]
    past_len = page_table.shape[1] * cache_dsa_kv.shape[1]
    n_keys = past_len + S
    q_pos = past_len + jnp.arange(S)
    ps = split_proj(rmsnorm(x_sample, ln_mix) @ w_in)
    q_ns = ps["nsa_q"].reshape(DB, S, NSA_HEADS, HEAD_DIM)
    g_ns = jax.nn.sigmoid(ps["nsa_gate"]).reshape(DB, S, NSA_HEADS, 3)
    kv6s = ps["nsa_kv"].reshape(DB, S, 6, NSA_KV, HEAD_DIM)
    cmp_full = jnp.concatenate([gather_pages(cache_nsa_cmp_kv, page_table), kv6s[:, :, 0:2]], axis=1)
    kc_s = compress(cmp_full[:, :, 0], cmp_k_pe, cmp_k_w1, cmp_k_w2)
    vc_s = compress(cmp_full[:, :, 1], cmp_v_pe, cmp_v_w1, cmp_v_w2)
    c_end_s = jnp.arange(kc_s.shape[1]) * CMP_STRIDE + CMP_LEN - 1
    new_sel = kv6s[:, :, 2:4]
    gi_s = jnp.arange(NSA_KV)[None, None, :, None]
    win_rows_s = jnp.concatenate([cache_nsa_win_kv, kv6s[:, :, 4:6]], axis=1)
    wb = cache_nsa_win_kv.shape[1]
    w_pos_s = past_len - wb + jnp.arange(wb + S)
    o_ns = nsa_attend(q_ns, q_pos, g_ns, kc_s, vc_s, c_end_s,
                      lambda tok: paged_rows(cache_nsa_sel_kv, page_table, new_sel, tok, past_len, gi_s),
                      n_keys, win_rows_s, w_pos_s, rel_nsa)
    q_ds = ps["dsa_q"].reshape(DB, S, DSA_HEADS, HEAD_DIM)
    dsa_rows_s = ps["dsa_kv"].reshape(DB, S, 2, DSA_KV, HEAD_DIM)
    ki_full = jnp.concatenate([gather_pages(cache_dsa_idx_k, page_table), ps["idx_k"]], axis=1)
    o_ds = dsa_attend(q_ds, q_pos, ps["idx_q"].reshape(DB, S, IDX_HEADS, IDX_DIM), ps["idx_w"], ki_full,
                      lambda idx: paged_rows(cache_dsa_kv, page_table, dsa_rows_s, idx, past_len), rel_dsa)
    y_sample = layer_tail(x_sample, o_ns, o_ds, ps["merge_gate"], cache_mem_kv, *tail_w)

    return (y_prompt, y_sample,
            kv6[:, :, 0:2], kv6[:, :, 2:4], kv6[:, L - min(WINDOW, L):, 4:6], dsa_rows, ki, mem_kv_p,
            kv6s[:, :, 0:2], kv6s[:, :, 2:4], win_rows_s[:, -wb:], dsa_rows_s, ps["idx_k"])
```

```python
import math
from functools import partial

import jax
import jax.numpy as jnp
import numpy as np
from jax import lax
from jax.experimental import pallas as pl
from jax.experimental.pallas import tpu as pltpu

D_MODEL = 1024
PAGE_SIZE = 128
HEAD_DIM = 64
NSA_HEADS = 8
NSA_KV = 2
NSA_REP = NSA_HEADS // NSA_KV
CMP_LEN = 32
CMP_STRIDE = 16
CMP_HIDDEN = 128
SEL_BLOCK = 64
N_SEL = 16
WINDOW = 512
FORCED_SCORE = 1e4
DSA_HEADS = 8
DSA_KV = 2
DSA_REP = DSA_HEADS // DSA_KV
IDX_HEADS = 4
IDX_DIM = 64
DSA_TOPK_MAX = 256
MEM_LEN = 256
X_HEADS = 4
X_HEAD_DIM = 64
N_EXPERTS = 32
TOP_K = 4
D_FF = 1024
SWIGLU_LIMIT = 7.0
SWIGLU_ALPHA = 1.702
MOE_BLOCK = 128
N_BUCKETS = 32
REL_MAX_DIST = 128
Q_BLOCK = 128
RMS_EPS = 1e-6
NEG_INF = -1e30
N_REL_HEADS = NSA_HEADS + DSA_HEADS
SPLITS = (("nsa_q", NSA_HEADS * HEAD_DIM), ("nsa_kv", 6 * NSA_KV * HEAD_DIM), ("nsa_gate", 3 * NSA_HEADS),
          ("dsa_q", DSA_HEADS * HEAD_DIM), ("dsa_kv", 2 * DSA_KV * HEAD_DIM), ("idx_q", IDX_HEADS * IDX_DIM),
          ("idx_k", IDX_DIM), ("idx_w", IDX_HEADS), ("merge_gate", 2 * D_MODEL))
D_IN = sum(n for _, n in SPLITS)

LANE = 128
VMEM_LIMIT = 48 * 1024 * 1024


def _norm_proj_body(x_ref, g_ref, w_ref, o_ref):
    x = x_ref[...]
    y = x * lax.rsqrt(jnp.mean(x * x, axis=-1, keepdims=True) + RMS_EPS) * g_ref[...]
    o_ref[...] = jnp.dot(y.astype(jnp.bfloat16), w_ref[...], preferred_element_type=jnp.float32)


def norm_proj(x, g, w, tm=512, tn=640):
    n, d = x.shape
    m = w.shape[1]
    m_pad = -(-m // tn) * tn
    wb = jnp.pad(w, ((0, 0), (0, m_pad - m))).astype(jnp.bfloat16)
    tm = min(tm, n)
    out = pl.pallas_call(
        _norm_proj_body,
        grid=(n // tm, m_pad // tn),
        in_specs=[pl.BlockSpec((tm, d), lambda i, j: (i, 0)),
                  pl.BlockSpec((1, d), lambda i, j: (0, 0)),
                  pl.BlockSpec((d, tn), lambda i, j: (0, j))],
        out_specs=pl.BlockSpec((tm, tn), lambda i, j: (i, j)),
        out_shape=jax.ShapeDtypeStruct((n, m_pad), jnp.float32),
        compiler_params=pltpu.CompilerParams(dimension_semantics=("arbitrary", "arbitrary"),
                                             vmem_limit_bytes=VMEM_LIMIT),
        name="norm_proj",
    )(x, g.reshape(1, d), wb)
    return out[:, :m]


def rmsnorm(x, g):
    xf = x.astype(jnp.float32)
    y = xf * lax.rsqrt(jnp.mean(xf * xf, axis=-1, keepdims=True) + RMS_EPS)
    return (y * g.astype(jnp.float32)).astype(x.dtype)


def split_proj(h):
    out, off = {}, 0
    for name, n in SPLITS:
        out[name] = h[..., off:off + n]
        off += n
    return out


def t5_bucket(dist):
    n = jnp.maximum(dist, 0)
    max_exact = N_BUCKETS // 2
    nf = jnp.maximum(n, 1).astype(jnp.float32)
    large = max_exact + (jnp.log(nf / max_exact) / math.log(REL_MAX_DIST / max_exact)
                         * (N_BUCKETS - max_exact)).astype(jnp.int32)
    return jnp.where(n < max_exact, n, jnp.minimum(large, N_BUCKETS - 1))


def masked_softmax(s, mask):
    p = jax.nn.softmax(jnp.where(mask, s, NEG_INF), axis=-1)
    return jnp.where(mask, p, 0.0)


def cmp_to_sel_matrix(nc, ns):
    c0 = np.arange(nc) * CMP_STRIDE
    s0 = np.arange(ns) * SEL_BLOCK
    ov = np.minimum(c0[:, None] + CMP_LEN, s0[None, :] + SEL_BLOCK) - np.maximum(c0[:, None], s0[None, :])
    return jnp.asarray(np.clip(ov, 0, None) / CMP_LEN, dtype=jnp.float32)


def compress(rows, pe, w1, w2):
    B, L, G, D = rows.shape
    r = CMP_LEN // CMP_STRIDE
    nch = L // CMP_STRIDE
    nc = nch - r + 1
    ch = rows[:, :nch * CMP_STRIDE].reshape(B, nch, CMP_STRIDE, G, D)
    w1c = w1.reshape(r, CMP_STRIDE, D, CMP_HIDDEN)
    hid = jnp.einsum('ld,ldh->h', pe, w1)
    for i in range(r):
        hid = hid + jnp.einsum('bnsgd,sdh->bngh', ch[:, i:i + nc], w1c[i])
    return jax.nn.silu(hid) @ w2


def gather_pages(pool, page_table):
    nb, npg = page_table.shape
    return pool[page_table].reshape((nb, npg * pool.shape[1]) + pool.shape[2:])


def paged_rows(pool, page_table, new_rows, pos, past_len, gi=None):
    nb = pos.shape[0]
    bi = jnp.arange(nb).reshape((nb,) + (1,) * (pos.ndim - 1))
    pp = jnp.minimum(pos, past_len - 1)
    phys = page_table[bi, pp // pool.shape[1]]
    off = pp % pool.shape[1]
    npos = jnp.clip(pos - past_len, 0, new_rows.shape[1] - 1)
    if gi is None:
        past, new = pool[phys, off], new_rows[bi, npos]
    else:
        past, new = pool[phys, off, :, gi], new_rows[bi, npos, :, gi]
    is_past = (pos < past_len).reshape(pos.shape + (1,) * (past.ndim - pos.ndim))
    return jnp.where(is_past, past, new)


def nsa_attend(q, q_pos, gates, kc, vc, c_end, gather_sel, n_keys, win_rows, w_pos, rel_tab):
    B, T = q.shape[:2]
    scale = HEAD_DIM ** -0.5
    qg = q.reshape(B, T, NSA_KV, NSA_REP, HEAD_DIM)
    d_c = q_pos[:, None] - c_end[None, :]
    s = jnp.einsum('btgrd,bcgd->btgrc', qg, kc).astype(jnp.float32) * scale
    s = s + jnp.moveaxis(rel_tab[t5_bucket(d_c)], 1, -1)
    p_c = masked_softmax(s, (d_c >= 0)[:, None, None, :])
    o_c = jnp.einsum('btgrc,bcgd->btgrd', p_c.astype(vc.dtype), vc)
    nc = kc.shape[1]
    ns = -(-n_keys // SEL_BLOCK)
    blk = jnp.einsum('btgrc,cj->btgj', p_c, cmp_to_sel_matrix(nc, ns))
    j = jnp.arange(ns)
    cur = (q_pos // SEL_BLOCK)[:, None]
    visible = ((j * SEL_BLOCK)[None, :] <= q_pos[:, None])[:, None, :]
    forced = ((j[None, :] == 0) | (j[None, :] == cur) | (j[None, :] == cur - 1))[:, None, :]
    score = jnp.where(visible, jnp.where(forced, FORCED_SCORE, blk), -jnp.inf)
    _, idx = lax.top_k(score, min(N_SEL, ns))
    tok = (idx[..., None] * SEL_BLOCK + jnp.arange(SEL_BLOCK)).reshape(B, T, NSA_KV, -1)
    rows = gather_sel(jnp.minimum(tok, n_keys - 1))
    d_s = q_pos[None, :, None, None] - tok
    s = jnp.einsum('btgrd,btgkd->btgrk', qg, rows[..., 0, :]).astype(jnp.float32) * scale
    bias = rel_tab[t5_bucket(d_s), jnp.arange(NSA_KV)[:, None]]
    p_s = masked_softmax(s + jnp.moveaxis(bias, -1, 3), (d_s >= 0)[:, :, :, None, :])
    o_s = jnp.einsum('btgrk,btgkd->btgrd', p_s.astype(rows.dtype), rows[..., 1, :])
    d_w = q_pos[:, None] - w_pos[None, :]
    s = jnp.einsum('btgrd,bwgd->btgrw', qg, win_rows[:, :, 0]).astype(jnp.float32) * scale
    s = s + jnp.moveaxis(rel_tab[t5_bucket(d_w)], 1, -1)
    ok_w = (d_w >= 0) & (d_w <= WINDOW) & (w_pos >= 0)[None, :]
    p_w = masked_softmax(s, ok_w[:, None, None, :])
    o_w = jnp.einsum('btgrw,bwgd->btgrd', p_w.astype(win_rows.dtype), win_rows[:, :, 1])
    g = gates.reshape(B, T, NSA_KV, NSA_REP, 3).astype(o_c.dtype)
    o = o_c * g[..., 0:1] + o_s * g[..., 1:2] + o_w * g[..., 2:3]
    return o.reshape(B, T, NSA_HEADS * HEAD_DIM)


def dsa_attend(q, q_pos, qi, wi, ki, gather_kv, rel_tab):
    B, T = q.shape[:2]
    n_keys = ki.shape[1]
    s_i = jnp.einsum('bthe,ble->bthl', qi, ki).astype(jnp.float32) * IDX_DIM ** -0.5
    score = jnp.einsum('bthl,bth->btl', jax.nn.relu(s_i), wi.astype(jnp.float32))
    admissible = jnp.arange(n_keys)[None, :] <= q_pos[:, None]
    score = jnp.where(admissible[None], score, -jnp.inf)
    _, idx = lax.top_k(score, min(DSA_TOPK_MAX, n_keys // 4))
    rows = gather_kv(idx)
    qg = q.reshape(B, T, DSA_KV, DSA_REP, HEAD_DIM)
    d = q_pos[None, :, None] - idx
    s = jnp.einsum('btgrd,btkgd->btgrk', qg, rows[:, :, :, 0]).astype(jnp.float32) * HEAD_DIM ** -0.5
    s = s + jnp.moveaxis(rel_tab[t5_bucket(d)], 2, -1)
    p = masked_softmax(s, (d >= 0)[:, :, None, None, :])
    o = jnp.einsum('btgrk,btkgd->btgrd', p.astype(rows.dtype), rows[:, :, :, 1])
    return o.reshape(B, T, DSA_HEADS * HEAD_DIM)


def mem_attend(h, mem_kv, w_q, w_o):
    B, T = h.shape[:2]
    q = (h @ w_q).reshape(B, T, X_HEADS, X_HEAD_DIM)
    s = jnp.einsum('bthd,bmhd->bthm', q, mem_kv[:, :, 0]).astype(jnp.float32) * X_HEAD_DIM ** -0.5
    p = jax.nn.softmax(s, axis=-1)
    o = jnp.einsum('bthm,bmhd->bthd', p.astype(mem_kv.dtype), mem_kv[:, :, 1])
    return o.reshape(B, T, X_HEADS * X_HEAD_DIM) @ w_o


def moe(h, w_router, b_router, w_gu, b_gu, w_down, b_down):
    shp = h.shape
    xt = h.reshape(-1, shp[-1])
    n = xt.shape[0]
    logits = (xt @ w_router).astype(jnp.float32) + b_router.astype(jnp.float32)
    top_v, top_e = lax.top_k(logits, TOP_K)
    gate = jax.nn.softmax(top_v, axis=-1)
    a = n * TOP_K
    e_flat = top_e.reshape(-1)
    t_flat = jnp.repeat(jnp.arange(n), TOP_K)
    order = jnp.argsort(e_flat)
    es, ts, ws = e_flat[order], t_flat[order], gate.reshape(-1)[order]
    counts = jnp.bincount(e_flat, length=N_EXPERTS)
    padded = (counts + MOE_BLOCK - 1) // MOE_BLOCK * MOE_BLOCK
    pad_end = jnp.cumsum(padded)
    pad_start = pad_end - padded
    raw_start = jnp.cumsum(counts) - counts
    dest = pad_start[es] + jnp.arange(a) - raw_start[es]
    n_blocks = -(-a // MOE_BLOCK) + N_EXPERTS
    slot_tok = jnp.zeros((n_blocks * MOE_BLOCK,), jnp.int32).at[dest].set(ts)
    slot_w = jnp.zeros((n_blocks * MOE_BLOCK,), jnp.float32).at[dest].set(ws)
    blk_e = jnp.minimum(jnp.searchsorted(pad_end, jnp.arange(n_blocks) * MOE_BLOCK, side='right'), N_EXPERTS - 1)

    def run_block(args):
        tok_b, w_b, e = args
        xb = xt[tok_b]
        gu = xb @ w_gu[e] + b_gu[e]
        g = jnp.minimum(gu[:, :D_FF], SWIGLU_LIMIT)
        u = jnp.clip(gu[:, D_FF:], -SWIGLU_LIMIT, SWIGLU_LIMIT)
        hb = (u + 1) * (g * jax.nn.sigmoid(SWIGLU_ALPHA * g))
        return (hb @ w_down[e] + b_down[e]) * w_b[:, None].astype(xb.dtype)

    yb = lax.map(run_block, (slot_tok.reshape(n_blocks, MOE_BLOCK), slot_w.reshape(n_blocks, MOE_BLOCK), blk_e))
    y = jnp.zeros_like(xt).at[slot_tok].add(yb.reshape(-1, shp[-1]))
    return y.reshape(shp)


def layer_tail(x, o_n, o_d, merge_gate, mem_kv, w_nsa_out, w_dsa_out, w_o, ln_mem, w_mem_q, w_mem_o,
               ln_ffn, w_router, b_router, w_gu, b_gu, w_down, b_down, ln_f):
    g = jax.nn.sigmoid(merge_gate)
    x = x + (g[..., :D_MODEL] * (o_n @ w_nsa_out) + g[..., D_MODEL:] * (o_d @ w_dsa_out)) @ w_o
    x = x + mem_attend(rmsnorm(x, ln_mem), mem_kv, w_mem_q, w_mem_o)
    x = x + moe(rmsnorm(x, ln_ffn), w_router, b_router, w_gu, b_gu, w_down, b_down)
    return rmsnorm(x, ln_f)


def kernel(x_prompt, x_sample, cache_nsa_cmp_kv, cache_nsa_sel_kv, cache_nsa_win_kv, cache_dsa_kv,
           cache_dsa_idx_k, cache_mem_kv, page_table, mem_prompt, ln_mix, w_in, cmp_k_pe, cmp_k_w1, cmp_k_w2,
           cmp_v_pe, cmp_v_w1, cmp_v_w2, rel_bias, w_nsa_out, w_dsa_out, w_o, ln_mem, w_mem_q, w_mem_kv,
           w_mem_o, ln_ffn, w_router, b_router, w_gu, b_gu, w_down, b_down, ln_f):
    rel_nsa = rel_bias[:, :NSA_HEADS].reshape(N_BUCKETS, NSA_KV, NSA_REP)
    rel_dsa = rel_bias[:, NSA_HEADS:].reshape(N_BUCKETS, DSA_KV, DSA_REP)
    tail_w = (w_nsa_out, w_dsa_out, w_o, ln_mem, w_mem_q, w_mem_o, ln_ffn, w_router, b_router,
              w_gu, b_gu, w_down, b_down, ln_f)

    B, L = x_prompt.shape[:2]
    pp = split_proj(norm_proj(x_prompt.reshape(B * L, D_MODEL), ln_mix, w_in).reshape(B, L, D_IN))
    q_n = pp["nsa_q"].reshape(B, L, NSA_HEADS, HEAD_DIM)
    g_n = jax.nn.sigmoid(pp["nsa_gate"]).reshape(B, L, NSA_HEADS, 3)
    kv6 = pp["nsa_kv"].reshape(B, L, 6, NSA_KV, HEAD_DIM)
    kc = compress(kv6[:, :, 0], cmp_k_pe, cmp_k_w1, cmp_k_w2)
    vc = compress(kv6[:, :, 1], cmp_v_pe, cmp_v_w1, cmp_v_w2)
    c_end = jnp.arange(kc.shape[1]) * CMP_STRIDE + CMP_LEN - 1
    sel_rows = kv6[:, :, 2:4]
    win_pad = jnp.pad(kv6[:, :, 4:6], ((0, 0), (WINDOW, 0), (0, 0), (0, 0), (0, 0)))
    q_d = pp["dsa_q"].reshape(B, L, DSA_HEADS, HEAD_DIM)
    dsa_rows = pp["dsa_kv"].reshape(B, L, 2, DSA_KV, HEAD_DIM)
    qi = pp["idx_q"].reshape(B, L, IDX_HEADS, IDX_DIM)
    ki = pp["idx_k"]
    wi = pp["idx_w"]
    bi4 = jnp.arange(B)[:, None, None, None]
    gi4 = jnp.arange(NSA_KV)[None, None, :, None]
    bi3 = jnp.arange(B)[:, None, None]

    def prompt_block(start):
        sl = lambda a: lax.dynamic_slice_in_dim(a, start, Q_BLOCK, axis=1)
        q_pos = start + jnp.arange(Q_BLOCK)
        win_rows = lax.dynamic_slice_in_dim(win_pad, start, Q_BLOCK + WINDOW, axis=1)
        w_pos = start - WINDOW + jnp.arange(Q_BLOCK + WINDOW)
        o_n = nsa_attend(sl(q_n), q_pos, sl(g_n), kc, vc, c_end, lambda tok: sel_rows[bi4, tok, :, gi4],
                         L, win_rows, w_pos, rel_nsa)
        o_d = dsa_attend(sl(q_d), q_pos, sl(qi), sl(wi), ki, lambda idx: dsa_rows[bi3, idx], rel_dsa)
        return o_n, o_d

    o_n, o_d = lax.map(prompt_block, jnp.arange(L // Q_BLOCK) * Q_BLOCK)
    o_n = jnp.moveaxis(o_n, 0, 1).reshape(B, L, -1)
    o_d = jnp.moveaxis(o_d, 0, 1).reshape(B, L, -1)
    mem_kv_p = (mem_prompt @ w_mem_kv).reshape(B, MEM_LEN, 2, X_HEADS, X_HEAD_DIM)
    y_prompt = layer_tail(x_prompt, o_n, o_d, pp["merge_gate"], mem_kv_p, *tail_w)

    DB, S = x_sample.shape[:2]
    past_len = page_table.shape[1] * cache_dsa_kv.shape[1]
    n_keys = past_len + S
    q_pos = past_len + jnp.arange(S)
    ps = split_proj(norm_proj(x_sample.reshape(DB * S, D_MODEL), ln_mix, w_in).reshape(DB, S, D_IN))
    q_ns = ps["nsa_q"].reshape(DB, S, NSA_HEADS, HEAD_DIM)
    g_ns = jax.nn.sigmoid(ps["nsa_gate"]).reshape(DB, S, NSA_HEADS, 3)
    kv6s = ps["nsa_kv"].reshape(DB, S, 6, NSA_KV, HEAD_DIM)
    cmp_full = jnp.concatenate([gather_pages(cache_nsa_cmp_kv, page_table), kv6s[:, :, 0:2]], axis=1)
    kc_s = compress(cmp_full[:, :, 0], cmp_k_pe, cmp_k_w1, cmp_k_w2)
    vc_s = compress(cmp_full[:, :, 1], cmp_v_pe, cmp_v_w1, cmp_v_w2)
    c_end_s = jnp.arange(kc_s.shape[1]) * CMP_STRIDE + CMP_LEN - 1
    new_sel = kv6s[:, :, 2:4]
    gi_s = jnp.arange(NSA_KV)[None, None, :, None]
    win_rows_s = jnp.concatenate([cache_nsa_win_kv, kv6s[:, :, 4:6]], axis=1)
    wb = cache_nsa_win_kv.shape[1]
    w_pos_s = past_len - wb + jnp.arange(wb + S)
    o_ns = nsa_attend(q_ns, q_pos, g_ns, kc_s, vc_s, c_end_s,
                      lambda tok: paged_rows(cache_nsa_sel_kv, page_table, new_sel, tok, past_len, gi_s),
                      n_keys, win_rows_s, w_pos_s, rel_nsa)
    q_ds = ps["dsa_q"].reshape(DB, S, DSA_HEADS, HEAD_DIM)
    dsa_rows_s = ps["dsa_kv"].reshape(DB, S, 2, DSA_KV, HEAD_DIM)
    ki_full = jnp.concatenate([gather_pages(cache_dsa_idx_k, page_table), ps["idx_k"]], axis=1)
    o_ds = dsa_attend(q_ds, q_pos, ps["idx_q"].reshape(DB, S, IDX_HEADS, IDX_DIM), ps["idx_w"], ki_full,
                      lambda idx: paged_rows(cache_dsa_kv, page_table, dsa_rows_s, idx, past_len), rel_dsa)
    y_sample = layer_tail(x_sample, o_ns, o_ds, ps["merge_gate"], cache_mem_kv, *tail_w)

    L_w = min(WINDOW, L)
    return (y_prompt, y_sample,
            kv6[:, :, 0:2], kv6[:, :, 2:4], kv6[:, L - L_w:, 4:6], dsa_rows, ki, mem_kv_p,
            kv6s[:, :, 0:2], kv6s[:, :, 2:4], win_rows_s[:, -wb:], dsa_rows_s, ps["idx_k"])
```

```python
import math
from functools import partial

import jax
import jax.numpy as jnp
import numpy as np
from jax import lax
from jax.experimental import pallas as pl
from jax.experimental.pallas import tpu as pltpu

D_MODEL = 1024
PAGE_SIZE = 128
HEAD_DIM = 64
NSA_HEADS = 8
NSA_KV = 2
NSA_REP = NSA_HEADS // NSA_KV
CMP_LEN = 32
CMP_STRIDE = 16
CMP_HIDDEN = 128
SEL_BLOCK = 64
N_SEL = 16
WINDOW = 512
FORCED_SCORE = 1e4
DSA_HEADS = 8
DSA_KV = 2
DSA_REP = DSA_HEADS // DSA_KV
IDX_HEADS = 4
IDX_DIM = 64
DSA_TOPK_MAX = 256
MEM_LEN = 256
X_HEADS = 4
X_HEAD_DIM = 64
N_EXPERTS = 32
TOP_K = 4
D_FF = 1024
SWIGLU_LIMIT = 7.0
SWIGLU_ALPHA = 1.702
MOE_BLOCK = 128
N_BUCKETS = 32
REL_MAX_DIST = 128
Q_BLOCK = 128
RMS_EPS = 1e-6
NEG_INF = -1e30
N_REL_HEADS = NSA_HEADS + DSA_HEADS
SPLITS = (("nsa_q", NSA_HEADS * HEAD_DIM), ("nsa_kv", 6 * NSA_KV * HEAD_DIM), ("nsa_gate", 3 * NSA_HEADS),
          ("dsa_q", DSA_HEADS * HEAD_DIM), ("dsa_kv", 2 * DSA_KV * HEAD_DIM), ("idx_q", IDX_HEADS * IDX_DIM),
          ("idx_k", IDX_DIM), ("idx_w", IDX_HEADS), ("merge_gate", 2 * D_MODEL))
D_IN = sum(n for _, n in SPLITS)

LANE = 128
VMEM_LIMIT = 48 * 1024 * 1024


def _norm_proj_body(x_ref, g_ref, w_ref, o_ref):
    x = x_ref[...]
    y = x * lax.rsqrt(jnp.mean(x * x, axis=-1, keepdims=True) + RMS_EPS) * g_ref[...]
    o_ref[...] = jnp.dot(y.astype(jnp.bfloat16), w_ref[...], preferred_element_type=jnp.float32)


def norm_proj(x, g, w, tm=512, tn=640):
    n, d = x.shape
    m = w.shape[1]
    m_pad = -(-m // tn) * tn
    wb = jnp.pad(w, ((0, 0), (0, m_pad - m))).astype(jnp.bfloat16)
    tm = min(tm, n)
    out = pl.pallas_call(
        _norm_proj_body,
        grid=(n // tm, m_pad // tn),
        in_specs=[pl.BlockSpec((tm, d), lambda i, j: (i, 0)),
                  pl.BlockSpec((1, d), lambda i, j: (0, 0)),
                  pl.BlockSpec((d, tn), lambda i, j: (0, j))],
        out_specs=pl.BlockSpec((tm, tn), lambda i, j: (i, j)),
        out_shape=jax.ShapeDtypeStruct((n, m_pad), jnp.float32),
        compiler_params=pltpu.CompilerParams(dimension_semantics=("arbitrary", "arbitrary"),
                                             vmem_limit_bytes=VMEM_LIMIT),
        name="norm_proj",
    )(x, g.reshape(1, d), wb)
    return out[:, :m]


MXU_DTYPE = jnp.bfloat16
TQ = 128
TK = 512
INT_MIN = -2 ** 31
M_INIT = -5e29
MAX_EXACT = N_BUCKETS // 2


def _dot_nt(a, b):
    return lax.dot_general(a, b, (((1,), (1,)), ((), ())), preferred_element_type=jnp.float32)


def _bucket_np(d):
    n = np.maximum(d, 0)
    nf = np.maximum(n, 1).astype(np.float64)
    large = MAX_EXACT + (np.log(nf / MAX_EXACT) / math.log(REL_MAX_DIST / MAX_EXACT)
                         * (N_BUCKETS - MAX_EXACT)).astype(np.int64)
    return np.where(n < MAX_EXACT, n, np.minimum(large, N_BUCKETS - 1))


def _bias_table(rel, dist):
    return jnp.take(rel.T, jnp.asarray(_bucket_np(dist), jnp.int32), axis=1)


def _flash_step(q, k, v, bias, madd, m, l, acc):
    s = _dot_nt(q, k) + bias + madd
    m_new = jnp.maximum(m, jnp.max(s, axis=1, keepdims=True))
    alpha = jnp.exp(m - m_new)
    p = jnp.exp(s - m_new)
    l = alpha * l + jnp.sum(p, axis=1, keepdims=True)
    acc = alpha * acc + jnp.dot(p.astype(v.dtype), v, preferred_element_type=jnp.float32)
    return m_new, l, acc


def _masked_branch(q, k_ref, v_ref, madd_scr, bias_last, bias_far, start, c_lo, nch):
    def body(c, carry):
        r0 = pl.multiple_of(start + c * TK, TQ)
        return _flash_step(q, k_ref[0, pl.ds(r0, TK), :], v_ref[0, pl.ds(r0, TK), :], bias_far, madd_scr[c], *carry)

    init = (jnp.full((TQ, 1), M_INIT, jnp.float32), jnp.zeros((TQ, 1), jnp.float32),
            jnp.zeros((TQ, LANE), jnp.float32))
    carry = lax.fori_loop(c_lo, nch - 1, body, init)
    r0 = pl.multiple_of(start + (nch - 1) * TK, TQ)
    m, l, acc = _flash_step(q, k_ref[0, pl.ds(r0, TK), :], v_ref[0, pl.ds(r0, TK), :], bias_last,
                            madd_scr[nch - 1], *carry)
    return acc / l


def _dsa_body(qi_ref, wi_ref, ki_ref, q_ref, k_ref, v_ref, bias_ref, o_ref, key_scr, madd_scr, *, L, topk):
    qb = pl.program_id(1)
    start = qb * TQ
    nch = L // TK
    off = L - TQ
    c_lo = (off - start) // TK
    t_io = lax.broadcasted_iota(jnp.int32, (TQ, TK), 0)
    j_io = lax.broadcasted_iota(jnp.int32, (TQ, TK), 1)
    lane_hi = lax.broadcasted_iota(jnp.int32, (TQ, LANE), 1) >= IDX_DIM
    wi = wi_ref[0]
    qis = []
    for pr in range(IDX_HEADS // 2):
        qp = qi_ref[0, :, LANE * pr:LANE * (pr + 1)]
        qis.append(jnp.where(lane_hi, jnp.zeros_like(qp), qp))
        qis.append(jnp.where(lane_hi, qp, jnp.zeros_like(qp)))

    def valid_chunk(c):
        kpos = j_io + (c * TK + start - off)
        return (kpos >= 0) & (kpos <= t_io + start)

    def score_chunk(c, carry):
        r0 = pl.multiple_of(start + c * TK, TQ)
        kk = ki_ref[0, pl.ds(r0, TK), :]
        sc = jnp.zeros((TQ, TK), jnp.float32)
        for h in range(IDX_HEADS):
            sc = sc + jnp.maximum(_dot_nt(qis[h], kk), 0.0) * wi[:, h:h + 1]
        sc = jnp.where(sc == 0.0, 0.0, sc)
        bits = lax.bitcast_convert_type(sc, jnp.int32)
        key = bits ^ ((bits >> 31) & 0x7FFFFFFF)
        key_scr[c] = jnp.where(valid_chunk(c), key, INT_MIN)
        return carry

    lax.fori_loop(c_lo, nch, score_chunk, 0)

    def count_ge(cand):
        def body(c, acc):
            x = jnp.where(key_scr[c] >= cand, 1.0, 0.0)
            for j in range(TK // LANE):
                acc = acc + x[:, LANE * j:LANE * (j + 1)]
            return acc
        acc = lax.fori_loop(c_lo, nch, body, jnp.zeros((TQ, LANE), jnp.float32))
        return jnp.sum(acc, axis=1, keepdims=True)

    def search(nbits, init):
        def body(i, cur):
            cand = cur + lax.shift_left(jnp.int32(1), nbits - 1 - i)
            return jnp.where(count_ge(cand) >= float(topk), cand, cur)
        return lax.fori_loop(0, nbits, body, init)

    thr = search(32, jnp.full((TQ, 1), INT_MIN, jnp.int32))

    nb2 = max((L - 1).bit_length(), 1)

    def rekey(c, carry):
        kv = key_scr[c]
        kw = j_io + c * TK
        key_scr[c] = jnp.where(kv > thr, 2 ** nb2, jnp.where(kv == thr, (L - 1) - kw, -1))
        return carry

    lax.fori_loop(c_lo, nch, rekey, 0)
    thr2 = search(nb2, jnp.zeros((TQ, 1), jnp.int32))

    def finalize(c, carry):
        madd_scr[c] = jnp.where(valid_chunk(c) & (key_scr[c] >= thr2), 0.0, NEG_INF)
        return carry

    lax.fori_loop(c_lo, nch, finalize, 0)

    for h in range(DSA_HEADS):
        q = q_ref[0, :, LANE * h:LANE * (h + 1)]
        o = _masked_branch(q, k_ref, v_ref, madd_scr, bias_ref[h, :, 0:TK], bias_ref[h, :, TK:TK + 1],
                           start, c_lo, nch)
        o_ref[0, h] = o.astype(o_ref.dtype)


def _nsa_body(q_ref, gate_ref, kc_ref, vc_ref, ks_ref, vs_ref, kw_ref, vw_ref, cb_ref, mfix_ref, e3_ref,
              sbias_ref, wbias_ref, o_ref, madd_scr, *, L, WC, NS, n_sel):
    qb = pl.program_id(1)
    start = qb * TQ
    nch = L // TK
    off = L - TQ
    c_lo = (off - start) // TK
    WW = WINDOW + TQ
    t_io = lax.broadcasted_iota(jnp.int32, (TQ, TK), 0)
    j_io = lax.broadcasted_iota(jnp.int32, (TQ, TK), 1)
    gates = jax.nn.sigmoid(gate_ref[0])

    r8 = pl.multiple_of(qb * (TQ // CMP_STRIDE), 8)
    kcw = kc_ref[0, pl.ds(r8, WC), :].astype(MXU_DTYPE)
    vcw = vc_ref[0, pl.ds(r8, WC), :].astype(MXU_DTYPE)
    t_c = lax.broadcasted_iota(jnp.int32, (TQ, WC), 0)
    u_c = lax.broadcasted_iota(jnp.int32, (TQ, WC), 1)
    c_idx = u_c + (qb * (TQ // CMP_STRIDE) - (WC - TQ // CMP_STRIDE))
    d_c = t_c - CMP_STRIDE * (u_c - (WC - TQ // CMP_STRIDE)) - (CMP_LEN - 1)
    valid_c = (c_idx >= 0) & (d_c >= 0)

    t_w = lax.broadcasted_iota(jnp.int32, (TQ, WW), 0)
    j_w = lax.broadcasted_iota(jnp.int32, (TQ, WW), 1)
    d_w = t_w + WINDOW - j_w
    valid_w = (d_w >= 0) & (d_w <= WINDOW) & (j_w + (start - WINDOW) >= 0)
    madd_w = jnp.where(valid_w, 0.0, NEG_INF)
    rw = pl.multiple_of(start, TQ)
    kwin = kw_ref[0, pl.ds(rw, WW), :]
    vwin = vw_ref[0, pl.ds(rw, WW), :]

    jj = lax.broadcasted_iota(jnp.int32, (TQ, NS), 1)
    qpos = lax.broadcasted_iota(jnp.int32, (TQ, NS), 0) + start
    j_abs = jj + (qb * (TQ // SEL_BLOCK) - (NS - TQ // SEL_BLOCK))
    cur = qpos // SEL_BLOCK
    visible = (j_abs >= 0) & (j_abs * SEL_BLOCK <= qpos)
    forced = (j_abs == 0) | (j_abs == cur) | (j_abs == cur - 1)
    mfix = mfix_ref[...]

    for g in range(NSA_KV):
        psum = jnp.zeros((TQ, WC), jnp.float32)
        o_cs = []
        for r in range(NSA_REP):
            h = g * NSA_REP + r
            q = q_ref[0, :, LANE * h:LANE * (h + 1)]
            s = jnp.where(valid_c, _dot_nt(q, kcw) + cb_ref[h], NEG_INF)
            m = jnp.max(s, axis=1, keepdims=True)
            p = jnp.where(valid_c, jnp.exp(s - m), 0.0)
            l = jnp.sum(p, axis=1, keepdims=True)
            pn = p * jnp.where(l > 0.0, 1.0 / l, 0.0)
            psum = psum + pn
            o_cs.append(jnp.dot(pn.astype(MXU_DTYPE), vcw, preferred_element_type=jnp.float32))
        h1 = psum.astype(jnp.bfloat16)
        r1 = psum - h1.astype(jnp.float32)
        h2 = r1.astype(jnp.bfloat16)
        h3 = (r1 - h2.astype(jnp.float32)).astype(jnp.bfloat16)
        blk = (jnp.dot(h1, mfix, preferred_element_type=jnp.float32)
               + jnp.dot(h2, mfix, preferred_element_type=jnp.float32)
               + jnp.dot(h3, mfix, preferred_element_type=jnp.float32))
        score = jnp.where(visible, jnp.where(forced, FORCED_SCORE, blk), -jnp.inf)
        rank = jnp.zeros((TQ, NS), jnp.float32)
        for i in range(NS):
            col = score[:, i:i + 1]
            rank = rank + jnp.where((col > score) | ((col == score) & (jj > i)), 1.0, 0.0)
        sel = jnp.where(rank < float(n_sel), 1.0, 0.0).astype(jnp.bfloat16)

        def expand(c, carry):
            me = jnp.dot(sel, e3_ref[c], preferred_element_type=jnp.float32)
            kpos = j_io + (c * TK + start - off)
            ok = (me > 0.5) & (kpos >= 0) & (kpos <= t_io + start)
            madd_scr[c] = jnp.where(ok, 0.0, NEG_INF)
            return carry

        lax.fori_loop(c_lo, nch, expand, 0)

        for r in range(NSA_REP):
            h = g * NSA_REP + r
            q = q_ref[0, :, LANE * h:LANE * (h + 1)]
            o_s = _masked_branch(q, ks_ref, vs_ref, madd_scr, sbias_ref[h, :, 0:TK], sbias_ref[h, :, TK:TK + 1],
                                 start, c_lo, nch)
            s = _dot_nt(q, kwin) + wbias_ref[h] + madd_w
            p = jnp.exp(s - jnp.max(s, axis=1, keepdims=True))
            o_w = (jnp.dot(p.astype(vwin.dtype), vwin, preferred_element_type=jnp.float32)
                   / jnp.sum(p, axis=1, keepdims=True))
            o = (o_cs[r] * gates[:, 3 * h:3 * h + 1] + o_s * gates[:, 3 * h + 1:3 * h + 2]
                 + o_w * gates[:, 3 * h + 2:3 * h + 3])
            o_ref[0, h] = o.astype(o_ref.dtype)


def _pad_heads(q, n_kv):
    B, L, hd = q.shape
    rep = hd // HEAD_DIM // n_kv
    eye = jnp.eye(n_kv, dtype=q.dtype).reshape(1, 1, n_kv, 1, n_kv, 1)
    return (q.reshape(B, L, n_kv, rep, 1, HEAD_DIM) * eye).reshape(B, L, n_kv * rep * n_kv * HEAD_DIM)


def _unpad_heads(o, n_kv):
    B, H, L, _ = o.shape
    rep = H // n_kv
    o6 = o.reshape(B, n_kv, rep, L, n_kv, HEAD_DIM)
    oo = jnp.stack([o6[:, g, :, :, g, :] for g in range(n_kv)], axis=1)
    return jnp.transpose(oo, (0, 3, 1, 2, 4)).reshape(B, L, H * HEAD_DIM)


def _last_chunk_bias(rel):
    t = np.arange(TQ)[:, None]
    j = np.arange(TK)[None, :]
    last = _bias_table(rel, t + TK - TQ - j)
    far = jnp.broadcast_to(rel[N_BUCKETS - 1][:, None, None], (rel.shape[1], TQ, LANE))
    return jnp.concatenate([last, far], axis=2)


def dsa_prompt(q_d, qi, wi, ki, k_d, v_d, rel):
    B, L, _ = q_d.shape
    assert L % TK == 0 and DSA_KV * HEAD_DIM == LANE
    topk = min(DSA_TOPK_MAX, L // 4)
    nch = L // TK
    front = ((0, 0), (L - TQ, 0), (0, 0))
    qp = _pad_heads(q_d * HEAD_DIM ** -0.5, DSA_KV).astype(MXU_DTYPE)
    ki2 = jnp.pad(jnp.concatenate([ki, ki], axis=-1), front).astype(MXU_DTYPE)
    kp = jnp.pad(k_d, front).astype(MXU_DTYPE)
    vp = jnp.pad(v_d, front).astype(MXU_DTYPE)
    bias = _last_chunk_bias(rel)
    rows = 2 * L - TQ
    out = pl.pallas_call(
        partial(_dsa_body, L=L, topk=topk),
        grid=(B, L // TQ),
        in_specs=[pl.BlockSpec((1, TQ, IDX_HEADS * IDX_DIM), lambda b, i: (b, i, 0)),
                  pl.BlockSpec((1, TQ, IDX_HEADS), lambda b, i: (b, i, 0)),
                  pl.BlockSpec((1, rows, LANE), lambda b, i: (b, 0, 0)),
                  pl.BlockSpec((1, TQ, DSA_HEADS * LANE), lambda b, i: (b, i, 0)),
                  pl.BlockSpec((1, rows, LANE), lambda b, i: (b, 0, 0)),
                  pl.BlockSpec((1, rows, LANE), lambda b, i: (b, 0, 0)),
                  pl.BlockSpec((DSA_HEADS, TQ, TK + LANE), lambda b, i: (0, 0, 0))],
        out_specs=pl.BlockSpec((1, DSA_HEADS, TQ, LANE), lambda b, i: (b, 0, i, 0)),
        out_shape=jax.ShapeDtypeStruct((B, DSA_HEADS, L, LANE), jnp.float32),
        scratch_shapes=[pltpu.VMEM((nch, TQ, TK), jnp.int32), pltpu.VMEM((nch, TQ, TK), jnp.float32)],
        compiler_params=pltpu.CompilerParams(dimension_semantics=("arbitrary", "arbitrary"),
                                             vmem_limit_bytes=VMEM_LIMIT),
        name="dsa_prompt",
    )(qi.astype(MXU_DTYPE) * IDX_DIM ** -0.5, wi, ki2, qp, kp, vp, bias)
    return _unpad_heads(out, DSA_KV)


def nsa_prompt(q_n, gate, kc, vc, k_s, v_s, k_w, v_w, rel):
    B, L, _ = q_n.shape
    assert L % TK == 0 and L >= WINDOW + TQ and NSA_KV * HEAD_DIM == LANE
    nch = L // TK
    NS = L // SEL_BLOCK
    n_sel = min(N_SEL, NS)
    NC = kc.shape[1]
    cq = TQ // CMP_STRIDE
    WC = max(-(-(L // CMP_STRIDE) // LANE) * LANE, LANE)
    rows_c = WC - cq + L // CMP_STRIDE
    qp = _pad_heads(q_n * HEAD_DIM ** -0.5, NSA_KV).astype(MXU_DTYPE)
    pad_c = ((0, 0), (WC - cq, rows_c - (WC - cq) - NC), (0, 0))
    kcp, vcp = jnp.pad(kc, pad_c), jnp.pad(vc, pad_c)
    front = ((0, 0), (L - TQ, 0), (0, 0))
    ksp, vsp = jnp.pad(k_s, front).astype(MXU_DTYPE), jnp.pad(v_s, front).astype(MXU_DTYPE)
    wfront = ((0, 0), (WINDOW, 0), (0, 0))
    kwp, vwp = jnp.pad(k_w, wfront).astype(MXU_DTYPE), jnp.pad(v_w, wfront).astype(MXU_DTYPE)
    t = np.arange(TQ)[:, None]
    cb = _bias_table(rel, t - CMP_STRIDE * (np.arange(WC)[None, :] - (WC - cq)) - (CMP_LEN - 1))
    c0 = (np.arange(WC) - (WC - cq)) * CMP_STRIDE
    s0 = (np.arange(NS) - (NS - TQ // SEL_BLOCK)) * SEL_BLOCK
    ov = np.minimum(c0[:, None] + CMP_LEN, s0[None, :] + SEL_BLOCK) - np.maximum(c0[:, None], s0[None, :])
    mfix = jnp.asarray(np.clip(ov, 0, None) / CMP_LEN, jnp.bfloat16)
    e3 = (np.arange(L)[None, :] // SEL_BLOCK == np.arange(NS)[:, None]).reshape(NS, nch, TK).transpose(1, 0, 2)
    e3 = jnp.asarray(e3, jnp.bfloat16)
    sbias = _last_chunk_bias(rel)
    WW = WINDOW + TQ
    wbias = _bias_table(rel, t + WINDOW - np.arange(WW)[None, :])
    rows = 2 * L - TQ
    full = lambda shp: pl.BlockSpec(shp, lambda b, i: (0,) * len(shp))
    per_b = lambda r: pl.BlockSpec((1, r, LANE), lambda b, i: (b, 0, 0))
    out = pl.pallas_call(
        partial(_nsa_body, L=L, WC=WC, NS=NS, n_sel=n_sel),
        grid=(B, L // TQ),
        in_specs=[pl.BlockSpec((1, TQ, NSA_HEADS * LANE), lambda b, i: (b, i, 0)),
                  pl.BlockSpec((1, TQ, 3 * NSA_HEADS), lambda b, i: (b, i, 0)),
                  per_b(rows_c), per_b(rows_c), per_b(rows), per_b(rows), per_b(L + WINDOW), per_b(L + WINDOW),
                  full((NSA_HEADS, TQ, WC)), full((WC, NS)), full((nch, NS, TK)),
                  full((NSA_HEADS, TQ, TK + LANE)), full((NSA_HEADS, TQ, WW))],
        out_specs=pl.BlockSpec((1, NSA_HEADS, TQ, LANE), lambda b, i: (b, 0, i, 0)),
        out_shape=jax.ShapeDtypeStruct((B, NSA_HEADS, L, LANE), jnp.float32),
        scratch_shapes=[pltpu.VMEM((nch, TQ, TK), jnp.float32)],
        compiler_params=pltpu.CompilerParams(dimension_semantics=("arbitrary", "arbitrary"),
                                             vmem_limit_bytes=VMEM_LIMIT),
        name="nsa_prompt",
    )(qp, gate, kcp, vcp, ksp, vsp, kwp, vwp, cb, mfix, e3, sbias, wbias)
    return _unpad_heads(out, NSA_KV)


def rmsnorm(x, g):
    xf = x.astype(jnp.float32)
    y = xf * lax.rsqrt(jnp.mean(xf * xf, axis=-1, keepdims=True) + RMS_EPS)
    return (y * g.astype(jnp.float32)).astype(x.dtype)


def split_proj(h):
    out, off = {}, 0
    for name, n in SPLITS:
        out[name] = h[..., off:off + n]
        off += n
    return out


def t5_bucket(dist):
    n = jnp.maximum(dist, 0)
    max_exact = N_BUCKETS // 2
    nf = jnp.maximum(n, 1).astype(jnp.float32)
    large = max_exact + (jnp.log(nf / max_exact) / math.log(REL_MAX_DIST / max_exact)
                         * (N_BUCKETS - max_exact)).astype(jnp.int32)
    return jnp.where(n < max_exact, n, jnp.minimum(large, N_BUCKETS - 1))


def masked_softmax(s, mask):
    p = jax.nn.softmax(jnp.where(mask, s, NEG_INF), axis=-1)
    return jnp.where(mask, p, 0.0)


def cmp_to_sel_matrix(nc, ns):
    c0 = np.arange(nc) * CMP_STRIDE
    s0 = np.arange(ns) * SEL_BLOCK
    ov = np.minimum(c0[:, None] + CMP_LEN, s0[None, :] + SEL_BLOCK) - np.maximum(c0[:, None], s0[None, :])
    return jnp.asarray(np.clip(ov, 0, None) / CMP_LEN, dtype=jnp.float32)


def compress(rows, pe, w1, w2):
    B, L, G, D = rows.shape
    r = CMP_LEN // CMP_STRIDE
    nch = L // CMP_STRIDE
    nc = nch - r + 1
    ch = rows[:, :nch * CMP_STRIDE].reshape(B, nch, CMP_STRIDE, G, D)
    w1c = w1.reshape(r, CMP_STRIDE, D, CMP_HIDDEN)
    hid = jnp.einsum('ld,ldh->h', pe, w1)
    for i in range(r):
        hid = hid + jnp.einsum('bnsgd,sdh->bngh', ch[:, i:i + nc], w1c[i])
    return jax.nn.silu(hid) @ w2


def gather_pages(pool, page_table):
    nb, npg = page_table.shape
    return pool[page_table].reshape((nb, npg * pool.shape[1]) + pool.shape[2:])


def paged_rows(pool, page_table, new_rows, pos, past_len, gi=None):
    nb = pos.shape[0]
    bi = jnp.arange(nb).reshape((nb,) + (1,) * (pos.ndim - 1))
    pp = jnp.minimum(pos, past_len - 1)
    phys = page_table[bi, pp // pool.shape[1]]
    off = pp % pool.shape[1]
    npos = jnp.clip(pos - past_len, 0, new_rows.shape[1] - 1)
    if gi is None:
        past, new = pool[phys, off], new_rows[bi, npos]
    else:
        past, new = pool[phys, off, :, gi], new_rows[bi, npos, :, gi]
    is_past = (pos < past_len).reshape(pos.shape + (1,) * (past.ndim - pos.ndim))
    return jnp.where(is_past, past, new)


def nsa_attend(q, q_pos, gates, kc, vc, c_end, gather_sel, n_keys, win_rows, w_pos, rel_tab):
    B, T = q.shape[:2]
    scale = HEAD_DIM ** -0.5
    qg = q.reshape(B, T, NSA_KV, NSA_REP, HEAD_DIM)
    d_c = q_pos[:, None] - c_end[None, :]
    s = jnp.einsum('btgrd,bcgd->btgrc', qg, kc).astype(jnp.float32) * scale
    s = s + jnp.moveaxis(rel_tab[t5_bucket(d_c)], 1, -1)
    p_c = masked_softmax(s, (d_c >= 0)[:, None, None, :])
    o_c = jnp.einsum('btgrc,bcgd->btgrd', p_c.astype(vc.dtype), vc)
    nc = kc.shape[1]
    ns = -(-n_keys // SEL_BLOCK)
    blk = jnp.einsum('btgrc,cj->btgj', p_c, cmp_to_sel_matrix(nc, ns))
    j = jnp.arange(ns)
    cur = (q_pos // SEL_BLOCK)[:, None]
    visible = ((j * SEL_BLOCK)[None, :] <= q_pos[:, None])[:, None, :]
    forced = ((j[None, :] == 0) | (j[None, :] == cur) | (j[None, :] == cur - 1))[:, None, :]
    score = jnp.where(visible, jnp.where(forced, FORCED_SCORE, blk), -jnp.inf)
    _, idx = lax.top_k(score, min(N_SEL, ns))
    tok = (idx[..., None] * SEL_BLOCK + jnp.arange(SEL_BLOCK)).reshape(B, T, NSA_KV, -1)
    rows = gather_sel(jnp.minimum(tok, n_keys - 1))
    d_s = q_pos[None, :, None, None] - tok
    s = jnp.einsum('btgrd,btgkd->btgrk', qg, rows[..., 0, :]).astype(jnp.float32) * scale
    bias = rel_tab[t5_bucket(d_s), jnp.arange(NSA_KV)[:, None]]
    p_s = masked_softmax(s + jnp.moveaxis(bias, -1, 3), (d_s >= 0)[:, :, :, None, :])
    o_s = jnp.einsum('btgrk,btgkd->btgrd', p_s.astype(rows.dtype), rows[..., 1, :])
    d_w = q_pos[:, None] - w_pos[None, :]
    s = jnp.einsum('btgrd,bwgd->btgrw', qg, win_rows[:, :, 0]).astype(jnp.float32) * scale
    s = s + jnp.moveaxis(rel_tab[t5_bucket(d_w)], 1, -1)
    ok_w = (d_w >= 0) & (d_w <= WINDOW) & (w_pos >= 0)[None, :]
    p_w = masked_softmax(s, ok_w[:, None, None, :])
    o_w = jnp.einsum('btgrw,bwgd->btgrd', p_w.astype(win_rows.dtype), win_rows[:, :, 1])
    g = gates.reshape(B, T, NSA_KV, NSA_REP, 3).astype(o_c.dtype)
    o = o_c * g[..., 0:1] + o_s * g[..., 1:2] + o_w * g[..., 2:3]
    return o.reshape(B, T, NSA_HEADS * HEAD_DIM)


def dsa_attend(q, q_pos, qi, wi, ki, gather_kv, rel_tab):
    B, T = q.shape[:2]
    n_keys = ki.shape[1]
    s_i = jnp.einsum('bthe,ble->bthl', qi, ki).astype(jnp.float32) * IDX_DIM ** -0.5
    score = jnp.einsum('bthl,bth->btl', jax.nn.relu(s_i), wi.astype(jnp.float32))
    admissible = jnp.arange(n_keys)[None, :] <= q_pos[:, None]
    score = jnp.where(admissible[None], score, -jnp.inf)
    _, idx = lax.top_k(score, min(DSA_TOPK_MAX, n_keys // 4))
    rows = gather_kv(idx)
    qg = q.reshape(B, T, DSA_KV, DSA_REP, HEAD_DIM)
    d = q_pos[None, :, None] - idx
    s = jnp.einsum('btgrd,btkgd->btgrk', qg, rows[:, :, :, 0]).astype(jnp.float32) * HEAD_DIM ** -0.5
    s = s + jnp.moveaxis(rel_tab[t5_bucket(d)], 2, -1)
    p = masked_softmax(s, (d >= 0)[:, :, None, None, :])
    o = jnp.einsum('btgrk,btkgd->btgrd', p.astype(rows.dtype), rows[:, :, :, 1])
    return o.reshape(B, T, DSA_HEADS * HEAD_DIM)


def mem_attend(h, mem_kv, w_q, w_o):
    B, T = h.shape[:2]
    q = (h @ w_q).reshape(B, T, X_HEADS, X_HEAD_DIM)
    s = jnp.einsum('bthd,bmhd->bthm', q, mem_kv[:, :, 0]).astype(jnp.float32) * X_HEAD_DIM ** -0.5
    p = jax.nn.softmax(s, axis=-1)
    o = jnp.einsum('bthm,bmhd->bthd', p.astype(mem_kv.dtype), mem_kv[:, :, 1])
    return o.reshape(B, T, X_HEADS * X_HEAD_DIM) @ w_o


def moe(h, w_router, b_router, w_gu, b_gu, w_down, b_down):
    shp = h.shape
    xt = h.reshape(-1, shp[-1])
    n = xt.shape[0]
    logits = (xt @ w_router).astype(jnp.float32) + b_router.astype(jnp.float32)
    top_v, top_e = lax.top_k(logits, TOP_K)
    gate = jax.nn.softmax(top_v, axis=-1)
    a = n * TOP_K
    e_flat = top_e.reshape(-1)
    t_flat = jnp.repeat(jnp.arange(n), TOP_K)
    order = jnp.argsort(e_flat)
    es, ts, ws = e_flat[order], t_flat[order], gate.reshape(-1)[order]
    counts = jnp.bincount(e_flat, length=N_EXPERTS)
    padded = (counts + MOE_BLOCK - 1) // MOE_BLOCK * MOE_BLOCK
    pad_end = jnp.cumsum(padded)
    pad_start = pad_end - padded
    raw_start = jnp.cumsum(counts) - counts
    dest = pad_start[es] + jnp.arange(a) - raw_start[es]
    n_blocks = -(-a // MOE_BLOCK) + N_EXPERTS
    slot_tok = jnp.zeros((n_blocks * MOE_BLOCK,), jnp.int32).at[dest].set(ts)
    slot_w = jnp.zeros((n_blocks * MOE_BLOCK,), jnp.float32).at[dest].set(ws)
    blk_e = jnp.minimum(jnp.searchsorted(pad_end, jnp.arange(n_blocks) * MOE_BLOCK, side='right'), N_EXPERTS - 1)

    def run_block(args):
        tok_b, w_b, e = args
        xb = xt[tok_b]
        gu = xb @ w_gu[e] + b_gu[e]
        g = jnp.minimum(gu[:, :D_FF], SWIGLU_LIMIT)
        u = jnp.clip(gu[:, D_FF:], -SWIGLU_LIMIT, SWIGLU_LIMIT)
        hb = (u + 1) * (g * jax.nn.sigmoid(SWIGLU_ALPHA * g))
        return (hb @ w_down[e] + b_down[e]) * w_b[:, None].astype(xb.dtype)

    yb = lax.map(run_block, (slot_tok.reshape(n_blocks, MOE_BLOCK), slot_w.reshape(n_blocks, MOE_BLOCK), blk_e))
    y = jnp.zeros_like(xt).at[slot_tok].add(yb.reshape(-1, shp[-1]))
    return y.reshape(shp)


def layer_tail(x, o_n, o_d, merge_gate, mem_kv, w_nsa_out, w_dsa_out, w_o, ln_mem, w_mem_q, w_mem_o,
               ln_ffn, w_router, b_router, w_gu, b_gu, w_down, b_down, ln_f):
    g = jax.nn.sigmoid(merge_gate)
    x = x + (g[..., :D_MODEL] * (o_n @ w_nsa_out) + g[..., D_MODEL:] * (o_d @ w_dsa_out)) @ w_o
    x = x + mem_attend(rmsnorm(x, ln_mem), mem_kv, w_mem_q, w_mem_o)
    x = x + moe(rmsnorm(x, ln_ffn), w_router, b_router, w_gu, b_gu, w_down, b_down)
    return rmsnorm(x, ln_f)


def kernel(x_prompt, x_sample, cache_nsa_cmp_kv, cache_nsa_sel_kv, cache_nsa_win_kv, cache_dsa_kv,
           cache_dsa_idx_k, cache_mem_kv, page_table, mem_prompt, ln_mix, w_in, cmp_k_pe, cmp_k_w1, cmp_k_w2,
           cmp_v_pe, cmp_v_w1, cmp_v_w2, rel_bias, w_nsa_out, w_dsa_out, w_o, ln_mem, w_mem_q, w_mem_kv,
           w_mem_o, ln_ffn, w_router, b_router, w_gu, b_gu, w_down, b_down, ln_f):
    rel_nsa = rel_bias[:, :NSA_HEADS].reshape(N_BUCKETS, NSA_KV, NSA_REP)
    rel_dsa = rel_bias[:, NSA_HEADS:].reshape(N_BUCKETS, DSA_KV, DSA_REP)
    tail_w = (w_nsa_out, w_dsa_out, w_o, ln_mem, w_mem_q, w_mem_o, ln_ffn, w_router, b_router,
              w_gu, b_gu, w_down, b_down, ln_f)

    B, L = x_prompt.shape[:2]
    pp = split_proj(norm_proj(x_prompt.reshape(B * L, D_MODEL), ln_mix, w_in).reshape(B, L, D_IN))
    kv6 = pp["nsa_kv"].reshape(B, L, 6, NSA_KV, HEAD_DIM)
    kc = compress(kv6[:, :, 0], cmp_k_pe, cmp_k_w1, cmp_k_w2)
    vc = compress(kv6[:, :, 1], cmp_v_pe, cmp_v_w1, cmp_v_w2)
    k6 = pp["nsa_kv"].reshape(B, L, 6, NSA_KV * HEAD_DIM)
    NC = kc.shape[1]
    o_n = nsa_prompt(pp["nsa_q"], pp["nsa_gate"], kc.reshape(B, NC, NSA_KV * HEAD_DIM),
                     vc.reshape(B, NC, NSA_KV * HEAD_DIM), k6[:, :, 2], k6[:, :, 3], k6[:, :, 4], k6[:, :, 5],
                     rel_bias[:, :NSA_HEADS])
    dsa_rows = pp["dsa_kv"].reshape(B, L, 2, DSA_KV, HEAD_DIM)
    dk = pp["dsa_kv"].reshape(B, L, 2, DSA_KV * HEAD_DIM)
    ki = pp["idx_k"]
    o_d = dsa_prompt(pp["dsa_q"], pp["idx_q"], pp["idx_w"], ki, dk[:, :, 0], dk[:, :, 1], rel_bias[:, NSA_HEADS:])
    mem_kv_p = (mem_prompt @ w_mem_kv).reshape(B, MEM_LEN, 2, X_HEADS, X_HEAD_DIM)
    y_prompt = layer_tail(x_prompt, o_n, o_d, pp["merge_gate"], mem_kv_p, *tail_w)

    DB, S = x_sample.shape[:2]
    past_len = page_table.shape[1] * cache_dsa_kv.shape[1]
    n_keys = past_len + S
    q_pos = past_len + jnp.arange(S)
    ps = split_proj(norm_proj(x_sample.reshape(DB * S, D_MODEL), ln_mix, w_in).reshape(DB, S, D_IN))
    q_ns = ps["nsa_q"].reshape(DB, S, NSA_HEADS, HEAD_DIM)
    g_ns = jax.nn.sigmoid(ps["nsa_gate"]).reshape(DB, S, NSA_HEADS, 3)
    kv6s = ps["nsa_kv"].reshape(DB, S, 6, NSA_KV, HEAD_DIM)
    cmp_full = jnp.concatenate([gather_pages(cache_nsa_cmp_kv, page_table), kv6s[:, :, 0:2]], axis=1)
    kc_s = compress(cmp_full[:, :, 0], cmp_k_pe, cmp_k_w1, cmp_k_w2)
    vc_s = compress(cmp_full[:, :, 1], cmp_v_pe, cmp_v_w1, cmp_v_w2)
    c_end_s = jnp.arange(kc_s.shape[1]) * CMP_STRIDE + CMP_LEN - 1
    new_sel = kv6s[:, :, 2:4]
    gi_s = jnp.arange(NSA_KV)[None, None, :, None]
    win_rows_s = jnp.concatenate([cache_nsa_win_kv, kv6s[:, :, 4:6]], axis=1)
    wb = cache_nsa_win_kv.shape[1]
    w_pos_s = past_len - wb + jnp.arange(wb + S)
    o_ns = nsa_attend(q_ns, q_pos, g_ns, kc_s, vc_s, c_end_s,
                      lambda tok: paged_rows(cache_nsa_sel_kv, page_table, new_sel, tok, past_len, gi_s),
                      n_keys, win_rows_s, w_pos_s, rel_nsa)
    q_ds = ps["dsa_q"].reshape(DB, S, DSA_HEADS, HEAD_DIM)
    dsa_rows_s = ps["dsa_kv"].reshape(DB, S, 2, DSA_KV, HEAD_DIM)
    ki_full = jnp.concatenate([gather_pages(cache_dsa_idx_k, page_table), ps["idx_k"]], axis=1)
    o_ds = dsa_attend(q_ds, q_pos, ps["idx_q"].reshape(DB, S, IDX_HEADS, IDX_DIM), ps["idx_w"], ki_full,
                      lambda idx: paged_rows(cache_dsa_kv, page_table, dsa_rows_s, idx, past_len), rel_dsa)
    y_sample = layer_tail(x_sample, o_ns, o_ds, ps["merge_gate"], cache_mem_kv, *tail_w)

    L_w = min(WINDOW, L)
    return (y_prompt, y_sample,
            kv6[:, :, 0:2], kv6[:, :, 2:4], kv6[:, L - L_w:, 4:6], dsa_rows, ki, mem_kv_p,
            kv6s[:, :, 0:2], kv6s[:, :, 2:4], win_rows_s[:, -wb:], dsa_rows_s, ps["idx_k"])
```

```python
import math
from functools import partial

import jax
import jax.numpy as jnp
import numpy as np
from jax import lax
from jax.experimental import pallas as pl
from jax.experimental.pallas import tpu as pltpu

D_MODEL = 1024
PAGE_SIZE = 128
HEAD_DIM = 64
NSA_HEADS = 8
NSA_KV = 2
NSA_REP = NSA_HEADS // NSA_KV
CMP_LEN = 32
CMP_STRIDE = 16
CMP_HIDDEN = 128
SEL_BLOCK = 64
N_SEL = 16
WINDOW = 512
FORCED_SCORE = 1e4
DSA_HEADS = 8
DSA_KV = 2
DSA_REP = DSA_HEADS // DSA_KV
IDX_HEADS = 4
IDX_DIM = 64
DSA_TOPK_MAX = 256
MEM_LEN = 256
X_HEADS = 4
X_HEAD_DIM = 64
N_EXPERTS = 32
TOP_K = 4
D_FF = 1024
SWIGLU_LIMIT = 7.0
SWIGLU_ALPHA = 1.702
MOE_BLOCK = 128
N_BUCKETS = 32
REL_MAX_DIST = 128
Q_BLOCK = 128
RMS_EPS = 1e-6
NEG_INF = -1e30
N_REL_HEADS = NSA_HEADS + DSA_HEADS
SPLITS = (("nsa_q", NSA_HEADS * HEAD_DIM), ("nsa_kv", 6 * NSA_KV * HEAD_DIM), ("nsa_gate", 3 * NSA_HEADS),
          ("dsa_q", DSA_HEADS * HEAD_DIM), ("dsa_kv", 2 * DSA_KV * HEAD_DIM), ("idx_q", IDX_HEADS * IDX_DIM),
          ("idx_k", IDX_DIM), ("idx_w", IDX_HEADS), ("merge_gate", 2 * D_MODEL))
D_IN = sum(n for _, n in SPLITS)

LANE = 128
VMEM_LIMIT = 48 * 1024 * 1024


def _norm_proj_body(x_ref, g_ref, w_ref, o_ref):
    x = x_ref[...]
    y = x * lax.rsqrt(jnp.mean(x * x, axis=-1, keepdims=True) + RMS_EPS) * g_ref[...]
    o_ref[...] = jnp.dot(y.astype(jnp.bfloat16), w_ref[...], preferred_element_type=jnp.float32)


def norm_proj(x, g, w, tm=512, tn=640):
    n, d = x.shape
    m = w.shape[1]
    m_pad = -(-m // tn) * tn
    wb = jnp.pad(w, ((0, 0), (0, m_pad - m))).astype(jnp.bfloat16)
    tm = min(tm, n)
    out = pl.pallas_call(
        _norm_proj_body,
        grid=(n // tm, m_pad // tn),
        in_specs=[pl.BlockSpec((tm, d), lambda i, j: (i, 0)),
                  pl.BlockSpec((1, d), lambda i, j: (0, 0)),
                  pl.BlockSpec((d, tn), lambda i, j: (0, j))],
        out_specs=pl.BlockSpec((tm, tn), lambda i, j: (i, j)),
        out_shape=jax.ShapeDtypeStruct((n, m_pad), jnp.float32),
        compiler_params=pltpu.CompilerParams(dimension_semantics=("arbitrary", "arbitrary"),
                                             vmem_limit_bytes=VMEM_LIMIT),
        name="norm_proj",
    )(x, g.reshape(1, d), wb)
    return out[:, :m]


MXU_DTYPE = jnp.bfloat16
TQ = 128
TK = 512
INT_MIN = -2 ** 31
M_INIT = -5e29
MAX_EXACT = N_BUCKETS // 2


def _dot_nt(a, b):
    return lax.dot_general(a, b, (((1,), (1,)), ((), ())), preferred_element_type=jnp.float32)


def _bucket_np(d):
    n = np.maximum(d, 0)
    nf = np.maximum(n, 1).astype(np.float64)
    large = MAX_EXACT + (np.log(nf / MAX_EXACT) / math.log(REL_MAX_DIST / MAX_EXACT)
                         * (N_BUCKETS - MAX_EXACT)).astype(np.int64)
    return np.where(n < MAX_EXACT, n, np.minimum(large, N_BUCKETS - 1))


def _bias_table(rel, dist):
    return jnp.take(rel.T, jnp.asarray(_bucket_np(dist), jnp.int32), axis=1)


def _flash_step(q, k, v, bias, madd, m, l, acc):
    s = _dot_nt(q, k) + bias + madd
    m_new = jnp.maximum(m, jnp.max(s, axis=1, keepdims=True))
    alpha = jnp.exp(m - m_new)
    p = jnp.exp(s - m_new)
    l = alpha * l + jnp.sum(p, axis=1, keepdims=True)
    acc = alpha * acc + jnp.dot(p.astype(v.dtype), v, preferred_element_type=jnp.float32)
    return m_new, l, acc


def _masked_branch(q, k_ref, v_ref, madd_scr, bias_last, bias_far, start, c_lo, nch):
    def body(c, carry):
        r0 = pl.multiple_of(start + c * TK, TQ)
        return _flash_step(q, k_ref[0, pl.ds(r0, TK), :], v_ref[0, pl.ds(r0, TK), :], bias_far, madd_scr[c], *carry)

    init = (jnp.full((TQ, 1), M_INIT, jnp.float32), jnp.zeros((TQ, 1), jnp.float32),
            jnp.zeros((TQ, LANE), jnp.float32))
    carry = lax.fori_loop(c_lo, nch - 1, body, init)
    r0 = pl.multiple_of(start + (nch - 1) * TK, TQ)
    m, l, acc = _flash_step(q, k_ref[0, pl.ds(r0, TK), :], v_ref[0, pl.ds(r0, TK), :], bias_last,
                            madd_scr[nch - 1], *carry)
    return acc / l


def _dsa_body(qi_ref, wi_ref, ki_ref, q_ref, k_ref, v_ref, bias_ref, o_ref, key_scr, madd_scr, *, L, topk):
    qb = pl.program_id(1)
    start = qb * TQ
    nch = L // TK
    off = L - TQ
    c_lo = (off - start) // TK
    t_io = lax.broadcasted_iota(jnp.int32, (TQ, TK), 0)
    j_io = lax.broadcasted_iota(jnp.int32, (TQ, TK), 1)
    lane_hi = lax.broadcasted_iota(jnp.int32, (TQ, LANE), 1) >= IDX_DIM
    wi = wi_ref[0]
    qis = []
    for pr in range(IDX_HEADS // 2):
        qp = qi_ref[0, :, LANE * pr:LANE * (pr + 1)]
        qis.append(jnp.where(lane_hi, jnp.zeros_like(qp), qp))
        qis.append(jnp.where(lane_hi, qp, jnp.zeros_like(qp)))

    def valid_chunk(c):
        kpos = j_io + (c * TK + start - off)
        return (kpos >= 0) & (kpos <= t_io + start)

    def score_chunk(c, carry):
        r0 = pl.multiple_of(start + c * TK, TQ)
        kk = ki_ref[0, pl.ds(r0, TK), :]
        sc = jnp.zeros((TQ, TK), jnp.float32)
        for h in range(IDX_HEADS):
            sc = sc + jnp.maximum(_dot_nt(qis[h], kk), 0.0) * wi[:, h:h + 1]
        sc = jnp.where(sc == 0.0, 0.0, sc)
        bits = lax.bitcast_convert_type(sc, jnp.int32)
        key = bits ^ ((bits >> 31) & 0x7FFFFFFF)
        key_scr[c] = jnp.where(valid_chunk(c), key, INT_MIN)
        return carry

    lax.fori_loop(c_lo, nch, score_chunk, 0)

    def count_ge(cand):
        def body(c, acc):
            x = jnp.where(key_scr[c] >= cand, 1.0, 0.0)
            for j in range(TK // LANE):
                acc = acc + x[:, LANE * j:LANE * (j + 1)]
            return acc
        acc = lax.fori_loop(c_lo, nch, body, jnp.zeros((TQ, LANE), jnp.float32))
        return jnp.sum(acc, axis=1, keepdims=True)

    def search(nbits, init):
        def body(i, cur):
            cand = cur + lax.shift_left(jnp.int32(1), nbits - 1 - i)
            return jnp.where(count_ge(cand) >= float(topk), cand, cur)
        return lax.fori_loop(0, nbits, body, init)

    thr = search(32, jnp.full((TQ, 1), INT_MIN, jnp.int32))

    nb2 = max((L - 1).bit_length(), 1)

    def rekey(c, carry):
        kv = key_scr[c]
        kw = j_io + c * TK
        key_scr[c] = jnp.where(kv > thr, 2 ** nb2, jnp.where(kv == thr, (L - 1) - kw, -1))
        return carry

    lax.fori_loop(c_lo, nch, rekey, 0)
    thr2 = search(nb2, jnp.zeros((TQ, 1), jnp.int32))

    def finalize(c, carry):
        madd_scr[c] = jnp.where(valid_chunk(c) & (key_scr[c] >= thr2), 0.0, NEG_INF)
        return carry

    lax.fori_loop(c_lo, nch, finalize, 0)

    for h in range(DSA_HEADS):
        q = q_ref[0, :, LANE * h:LANE * (h + 1)]
        o = _masked_branch(q, k_ref, v_ref, madd_scr, bias_ref[h, :, 0:TK], bias_ref[h, :, TK:TK + 1],
                           start, c_lo, nch)
        o_ref[0, h] = o.astype(o_ref.dtype)


def _nsa_body(q_ref, gate_ref, kc_ref, vc_ref, ks_ref, vs_ref, kw_ref, vw_ref, cb_ref, mfix_ref, e3_ref,
              sbias_ref, wbias_ref, o_ref, madd_scr, *, L, WC, NS, n_sel):
    qb = pl.program_id(1)
    start = qb * TQ
    nch = L // TK
    off = L - TQ
    c_lo = (off - start) // TK
    WW = WINDOW + TQ
    t_io = lax.broadcasted_iota(jnp.int32, (TQ, TK), 0)
    j_io = lax.broadcasted_iota(jnp.int32, (TQ, TK), 1)
    gates = jax.nn.sigmoid(gate_ref[0])

    r8 = pl.multiple_of(qb * (TQ // CMP_STRIDE), 8)
    kcw = kc_ref[0, pl.ds(r8, WC), :].astype(MXU_DTYPE)
    vcw = vc_ref[0, pl.ds(r8, WC), :].astype(MXU_DTYPE)
    t_c = lax.broadcasted_iota(jnp.int32, (TQ, WC), 0)
    u_c = lax.broadcasted_iota(jnp.int32, (TQ, WC), 1)
    c_idx = u_c + (qb * (TQ // CMP_STRIDE) - (WC - TQ // CMP_STRIDE))
    d_c = t_c - CMP_STRIDE * (u_c - (WC - TQ // CMP_STRIDE)) - (CMP_LEN - 1)
    valid_c = (c_idx >= 0) & (d_c >= 0)

    t_w = lax.broadcasted_iota(jnp.int32, (TQ, WW), 0)
    j_w = lax.broadcasted_iota(jnp.int32, (TQ, WW), 1)
    d_w = t_w + WINDOW - j_w
    valid_w = (d_w >= 0) & (d_w <= WINDOW) & (j_w + (start - WINDOW) >= 0)
    madd_w = jnp.where(valid_w, 0.0, NEG_INF)
    rw = pl.multiple_of(start, TQ)
    kwin = kw_ref[0, pl.ds(rw, WW), :]
    vwin = vw_ref[0, pl.ds(rw, WW), :]

    jj = lax.broadcasted_iota(jnp.int32, (TQ, NS), 1)
    qpos = lax.broadcasted_iota(jnp.int32, (TQ, NS), 0) + start
    j_abs = jj + (qb * (TQ // SEL_BLOCK) - (NS - TQ // SEL_BLOCK))
    cur = qpos // SEL_BLOCK
    visible = (j_abs >= 0) & (j_abs * SEL_BLOCK <= qpos)
    forced = (j_abs == 0) | (j_abs == cur) | (j_abs == cur - 1)
    mfix = mfix_ref[...]

    for g in range(NSA_KV):
        psum = jnp.zeros((TQ, WC), jnp.float32)
        o_cs = []
        for r in range(NSA_REP):
            h = g * NSA_REP + r
            q = q_ref[0, :, LANE * h:LANE * (h + 1)]
            s = jnp.where(valid_c, _dot_nt(q, kcw) + cb_ref[h], NEG_INF)
            m = jnp.max(s, axis=1, keepdims=True)
            p = jnp.where(valid_c, jnp.exp(s - m), 0.0)
            l = jnp.sum(p, axis=1, keepdims=True)
            pn = p * jnp.where(l > 0.0, 1.0 / l, 0.0)
            psum = psum + pn
            o_cs.append(jnp.dot(pn.astype(MXU_DTYPE), vcw, preferred_element_type=jnp.float32))
        h1 = psum.astype(jnp.bfloat16)
        r1 = psum - h1.astype(jnp.float32)
        h2 = r1.astype(jnp.bfloat16)
        h3 = (r1 - h2.astype(jnp.float32)).astype(jnp.bfloat16)
        blk = (jnp.dot(h1, mfix, preferred_element_type=jnp.float32)
               + jnp.dot(h2, mfix, preferred_element_type=jnp.float32)
               + jnp.dot(h3, mfix, preferred_element_type=jnp.float32))
        score = jnp.where(visible, jnp.where(forced, FORCED_SCORE, blk), -jnp.inf)
        rank = jnp.zeros((TQ, NS), jnp.float32)
        for i in range(NS):
            col = score[:, i:i + 1]
            rank = rank + jnp.where((col > score) | ((col == score) & (jj > i)), 1.0, 0.0)
        sel = jnp.where(rank < float(n_sel), 1.0, 0.0).astype(jnp.bfloat16)

        def expand(c, carry):
            me = jnp.dot(sel, e3_ref[c], preferred_element_type=jnp.float32)
            kpos = j_io + (c * TK + start - off)
            ok = (me > 0.5) & (kpos >= 0) & (kpos <= t_io + start)
            madd_scr[c] = jnp.where(ok, 0.0, NEG_INF)
            return carry

        lax.fori_loop(c_lo, nch, expand, 0)

        for r in range(NSA_REP):
            h = g * NSA_REP + r
            q = q_ref[0, :, LANE * h:LANE * (h + 1)]
            o_s = _masked_branch(q, ks_ref, vs_ref, madd_scr, sbias_ref[h, :, 0:TK], sbias_ref[h, :, TK:TK + 1],
                                 start, c_lo, nch)
            s = _dot_nt(q, kwin) + wbias_ref[h] + madd_w
            p = jnp.exp(s - jnp.max(s, axis=1, keepdims=True))
            o_w = (jnp.dot(p.astype(vwin.dtype), vwin, preferred_element_type=jnp.float32)
                   / jnp.sum(p, axis=1, keepdims=True))
            o = (o_cs[r] * gates[:, 3 * h:3 * h + 1] + o_s * gates[:, 3 * h + 1:3 * h + 2]
                 + o_w * gates[:, 3 * h + 2:3 * h + 3])
            o_ref[0, h] = o.astype(o_ref.dtype)


def _pad_heads(q, n_kv):
    B, L, hd = q.shape
    rep = hd // HEAD_DIM // n_kv
    eye = jnp.eye(n_kv, dtype=q.dtype).reshape(1, 1, n_kv, 1, n_kv, 1)
    return (q.reshape(B, L, n_kv, rep, 1, HEAD_DIM) * eye).reshape(B, L, n_kv * rep * n_kv * HEAD_DIM)


def _unpad_heads(o, n_kv):
    B, H, L, _ = o.shape
    rep = H // n_kv
    o6 = o.reshape(B, n_kv, rep, L, n_kv, HEAD_DIM)
    oo = jnp.stack([o6[:, g, :, :, g, :] for g in range(n_kv)], axis=1)
    return jnp.transpose(oo, (0, 3, 1, 2, 4)).reshape(B, L, H * HEAD_DIM)


def _last_chunk_bias(rel):
    t = np.arange(TQ)[:, None]
    j = np.arange(TK)[None, :]
    last = _bias_table(rel, t + TK - TQ - j)
    far = jnp.broadcast_to(rel[N_BUCKETS - 1][:, None, None], (rel.shape[1], TQ, LANE))
    return jnp.concatenate([last, far], axis=2)


def dsa_prompt(q_d, qi, wi, ki, k_d, v_d, rel):
    B, L, _ = q_d.shape
    assert L % TK == 0 and DSA_KV * HEAD_DIM == LANE
    topk = min(DSA_TOPK_MAX, L // 4)
    nch = L // TK
    front = ((0, 0), (L - TQ, 0), (0, 0))
    qp = _pad_heads(q_d * HEAD_DIM ** -0.5, DSA_KV).astype(MXU_DTYPE)
    ki2 = jnp.pad(jnp.concatenate([ki, ki], axis=-1), front).astype(MXU_DTYPE)
    kp = jnp.pad(k_d, front).astype(MXU_DTYPE)
    vp = jnp.pad(v_d, front).astype(MXU_DTYPE)
    bias = _last_chunk_bias(rel)
    rows = 2 * L - TQ
    out = pl.pallas_call(
        partial(_dsa_body, L=L, topk=topk),
        grid=(B, L // TQ),
        in_specs=[pl.BlockSpec((1, TQ, IDX_HEADS * IDX_DIM), lambda b, i: (b, i, 0)),
                  pl.BlockSpec((1, TQ, IDX_HEADS), lambda b, i: (b, i, 0)),
                  pl.BlockSpec((1, rows, LANE), lambda b, i: (b, 0, 0)),
                  pl.BlockSpec((1, TQ, DSA_HEADS * LANE), lambda b, i: (b, i, 0)),
                  pl.BlockSpec((1, rows, LANE), lambda b, i: (b, 0, 0)),
                  pl.BlockSpec((1, rows, LANE), lambda b, i: (b, 0, 0)),
                  pl.BlockSpec((DSA_HEADS, TQ, TK + LANE), lambda b, i: (0, 0, 0))],
        out_specs=pl.BlockSpec((1, DSA_HEADS, TQ, LANE), lambda b, i: (b, 0, i, 0)),
        out_shape=jax.ShapeDtypeStruct((B, DSA_HEADS, L, LANE), jnp.float32),
        scratch_shapes=[pltpu.VMEM((nch, TQ, TK), jnp.int32), pltpu.VMEM((nch, TQ, TK), jnp.float32)],
        compiler_params=pltpu.CompilerParams(dimension_semantics=("arbitrary", "arbitrary"),
                                             vmem_limit_bytes=VMEM_LIMIT),
        name="dsa_prompt",
    )(qi.astype(MXU_DTYPE) * IDX_DIM ** -0.5, wi, ki2, qp, kp, vp, bias)
    return _unpad_heads(out, DSA_KV)


def nsa_prompt(q_n, gate, kc, vc, k_s, v_s, k_w, v_w, rel):
    B, L, _ = q_n.shape
    assert L % TK == 0 and L >= WINDOW + TQ and NSA_KV * HEAD_DIM == LANE
    nch = L // TK
    NS = L // SEL_BLOCK
    n_sel = min(N_SEL, NS)
    NC = kc.shape[1]
    cq = TQ // CMP_STRIDE
    WC = max(-(-(L // CMP_STRIDE) // LANE) * LANE, LANE)
    rows_c = WC - cq + L // CMP_STRIDE
    qp = _pad_heads(q_n * HEAD_DIM ** -0.5, NSA_KV).astype(MXU_DTYPE)
    pad_c = ((0, 0), (WC - cq, rows_c - (WC - cq) - NC), (0, 0))
    kcp, vcp = jnp.pad(kc, pad_c), jnp.pad(vc, pad_c)
    front = ((0, 0), (L - TQ, 0), (0, 0))
    ksp, vsp = jnp.pad(k_s, front).astype(MXU_DTYPE), jnp.pad(v_s, front).astype(MXU_DTYPE)
    wfront = ((0, 0), (WINDOW, 0), (0, 0))
    kwp, vwp = jnp.pad(k_w, wfront).astype(MXU_DTYPE), jnp.pad(v_w, wfront).astype(MXU_DTYPE)
    t = np.arange(TQ)[:, None]
    cb = _bias_table(rel, t - CMP_STRIDE * (np.arange(WC)[None, :] - (WC - cq)) - (CMP_LEN - 1))
    c0 = (np.arange(WC) - (WC - cq)) * CMP_STRIDE
    s0 = (np.arange(NS) - (NS - TQ // SEL_BLOCK)) * SEL_BLOCK
    ov = np.minimum(c0[:, None] + CMP_LEN, s0[None, :] + SEL_BLOCK) - np.maximum(c0[:, None], s0[None, :])
    mfix = jnp.asarray(np.clip(ov, 0, None) / CMP_LEN, jnp.bfloat16)
    e3 = (np.arange(L)[None, :] // SEL_BLOCK == np.arange(NS)[:, None]).reshape(NS, nch, TK).transpose(1, 0, 2)
    e3 = jnp.asarray(e3, jnp.bfloat16)
    sbias = _last_chunk_bias(rel)
    WW = WINDOW + TQ
    wbias = _bias_table(rel, t + WINDOW - np.arange(WW)[None, :])
    rows = 2 * L - TQ
    full = lambda shp: pl.BlockSpec(shp, lambda b, i: (0,) * len(shp))
    per_b = lambda r: pl.BlockSpec((1, r, LANE), lambda b, i: (b, 0, 0))
    out = pl.pallas_call(
        partial(_nsa_body, L=L, WC=WC, NS=NS, n_sel=n_sel),
        grid=(B, L // TQ),
        in_specs=[pl.BlockSpec((1, TQ, NSA_HEADS * LANE), lambda b, i: (b, i, 0)),
                  pl.BlockSpec((1, TQ, 3 * NSA_HEADS), lambda b, i: (b, i, 0)),
                  per_b(rows_c), per_b(rows_c), per_b(rows), per_b(rows), per_b(L + WINDOW), per_b(L + WINDOW),
                  full((NSA_HEADS, TQ, WC)), full((WC, NS)), full((nch, NS, TK)),
                  full((NSA_HEADS, TQ, TK + LANE)), full((NSA_HEADS, TQ, WW))],
        out_specs=pl.BlockSpec((1, NSA_HEADS, TQ, LANE), lambda b, i: (b, 0, i, 0)),
        out_shape=jax.ShapeDtypeStruct((B, NSA_HEADS, L, LANE), jnp.float32),
        scratch_shapes=[pltpu.VMEM((nch, TQ, TK), jnp.float32)],
        compiler_params=pltpu.CompilerParams(dimension_semantics=("arbitrary", "arbitrary"),
                                             vmem_limit_bytes=VMEM_LIMIT),
        name="nsa_prompt",
    )(qp, gate, kcp, vcp, ksp, vsp, kwp, vwp, cb, mfix, e3, sbias, wbias)
    return _unpad_heads(out, NSA_KV)


TM_R = 256
TM_E = 256
TM_C = 128


def _router_body(x_ref, g_ref, wr_ref, br_ref, h_ref, meta_ref, cnt_ref, carry_scr):
    i = pl.program_id(0)

    @pl.when(i == 0)
    def _():
        carry_scr[...] = jnp.zeros_like(carry_scr)

    x = x_ref[...]
    h = x * lax.rsqrt(jnp.mean(x * x, axis=-1, keepdims=True) + RMS_EPS) * g_ref[...]
    h_ref[...] = h
    lane = lax.broadcasted_iota(jnp.int32, (TM_R, LANE), 1)
    lane_f = lane.astype(jnp.float32)
    logits = jnp.dot(h.astype(MXU_DTYPE), wr_ref[...], preferred_element_type=jnp.float32) + br_ref[...]
    cur = jnp.where(lane < N_EXPERTS, logits, -jnp.inf)
    vals, idxs = [], []
    for _ in range(TOP_K):
        m = jnp.max(cur, axis=1, keepdims=True)
        idx = jnp.min(jnp.where(cur == m, lane_f, float(LANE)), axis=1, keepdims=True)
        vals.append(m)
        idxs.append(idx)
        cur = jnp.where(lane_f == idx, -jnp.inf, cur)
    es = [jnp.exp(v - vals[0]) for v in vals]
    denom = es[0]
    for e in es[1:]:
        denom = denom + e
    onehots = [jnp.where(lane_f == idx, 1.0, 0.0) for idx in idxs]
    osum = onehots[0]
    for o in onehots[1:]:
        osum = osum + o
    r_io = lax.broadcasted_iota(jnp.int32, (TM_R, TM_R), 0)
    c_io = lax.broadcasted_iota(jnp.int32, (TM_R, TM_R), 1)
    ltri = jnp.where(c_io < r_io, 1.0, 0.0).astype(jnp.bfloat16)
    prefix = jnp.dot(ltri, osum.astype(jnp.bfloat16), preferred_element_type=jnp.float32) + carry_scr[...]
    meta = jnp.zeros((TM_R, LANE), jnp.float32)
    for k in range(TOP_K):
        pos = jnp.sum(onehots[k] * prefix, axis=1, keepdims=True)
        meta = jnp.where(lane == k, idxs[k], meta)
        meta = jnp.where(lane == TOP_K + k, pos, meta)
        meta = jnp.where(lane == 2 * TOP_K + k, es[k] / denom, meta)
    meta_ref[...] = meta
    carry_scr[...] = carry_scr[...] + jnp.sum(osum, axis=0, keepdims=True)
    cnt_ref[...] = carry_scr[...]


def _scatter_body(dest_ref, h_ref, xs_in_ref, xs_ref, sem):
    del xs_in_ref
    n = TM_R * TOP_K

    def row_copy(j):
        return pltpu.make_async_copy(h_ref.at[pl.ds(j // TOP_K, 1), :], xs_ref.at[pl.ds(dest_ref[j], 1), :], sem)

    def issue(j, c):
        row_copy(j).start()
        return c

    def drain(j, c):
        row_copy(j).wait()
        return c

    lax.fori_loop(0, n, issue, 0)
    lax.fori_loop(0, n, drain, 0)


def _expert_body(blk_e_ref, nblk_ref, xs_ref, wgu_ref, bgu_ref, wd_ref, bd_ref, ys_ref):
    del blk_e_ref
    i = pl.program_id(0)

    @pl.when(i < nblk_ref[0])
    def _():
        gu = jnp.dot(xs_ref[...].astype(MXU_DTYPE), wgu_ref[0], preferred_element_type=jnp.float32) + bgu_ref[0]
        g = jnp.minimum(gu[:, :D_FF], SWIGLU_LIMIT)
        u = jnp.clip(gu[:, D_FF:], -SWIGLU_LIMIT, SWIGLU_LIMIT)
        hb = (u + 1.0) * (g * jax.nn.sigmoid(SWIGLU_ALPHA * g))
        ys_ref[...] = jnp.dot(hb.astype(MXU_DTYPE), wd_ref[0], preferred_element_type=jnp.float32) + bd_ref[0]

    @pl.when(i >= nblk_ref[0])
    def _():
        ys_ref[...] = jnp.zeros_like(ys_ref)


def _combine_body(dest_ref, meta_ref, x_ref, g_ref, ys_ref, y_ref, buf, sem):
    n = TM_C * TOP_K

    def row_copy(j):
        return pltpu.make_async_copy(ys_ref.at[pl.ds(dest_ref[j], 1), :],
                                     buf.at[j % TOP_K, pl.ds(j // TOP_K, 1), :], sem)

    def issue(j, c):
        row_copy(j).start()
        return c

    def drain(j, c):
        row_copy(j).wait()
        return c

    lax.fori_loop(0, n, issue, 0)
    lax.fori_loop(0, n, drain, 0)
    acc = x_ref[...]
    meta = meta_ref[...]
    for k in range(TOP_K):
        acc = acc + meta[:, 2 * TOP_K + k:2 * TOP_K + k + 1] * buf[k]
    y_ref[...] = acc * lax.rsqrt(jnp.mean(acc * acc, axis=-1, keepdims=True) + RMS_EPS) * g_ref[...]


def moe_block(x, ln_ffn, w_router, b_router, w_gu, b_gu, w_down, b_down, ln_f):
    n, d = x.shape
    assert n % TM_R == 0 and n % TM_C == 0 and N_EXPERTS <= LANE
    cparams = pltpu.CompilerParams(dimension_semantics=("arbitrary",), vmem_limit_bytes=VMEM_LIMIT)
    wr = jnp.pad(w_router, ((0, 0), (0, LANE - N_EXPERTS))).astype(MXU_DTYPE)
    br = jnp.pad(b_router, (0, LANE - N_EXPERTS)).reshape(1, LANE)
    h, meta, cnt = pl.pallas_call(
        _router_body,
        grid=(n // TM_R,),
        in_specs=[pl.BlockSpec((TM_R, d), lambda i: (i, 0)), pl.BlockSpec((1, d), lambda i: (0, 0)),
                  pl.BlockSpec((d, LANE), lambda i: (0, 0)), pl.BlockSpec((1, LANE), lambda i: (0, 0))],
        out_specs=[pl.BlockSpec((TM_R, d), lambda i: (i, 0)), pl.BlockSpec((TM_R, LANE), lambda i: (i, 0)),
                   pl.BlockSpec((1, LANE), lambda i: (0, 0))],
        out_shape=[jax.ShapeDtypeStruct((n, d), jnp.float32), jax.ShapeDtypeStruct((n, LANE), jnp.float32),
                   jax.ShapeDtypeStruct((1, LANE), jnp.float32)],
        scratch_shapes=[pltpu.VMEM((1, LANE), jnp.float32)],
        compiler_params=cparams, name="moe_router",
    )(x, ln_ffn.reshape(1, d), wr, br)

    top_e = meta[:, :TOP_K].astype(jnp.int32)
    pos = meta[:, TOP_K:2 * TOP_K].astype(jnp.int32)
    counts = cnt[0, :N_EXPERTS].astype(jnp.int32)
    padded = (counts + TM_E - 1) // TM_E * TM_E
    pad_end = jnp.cumsum(padded)
    dest = ((pad_end - padded)[top_e] + pos).reshape(-1)
    n_blocks = n * TOP_K // TM_E + N_EXPERTS
    blk_e = jnp.minimum(jnp.searchsorted(pad_end, jnp.arange(n_blocks) * TM_E, side='right'),
                        N_EXPERTS - 1).astype(jnp.int32)
    n_used = (pad_end[-1] // TM_E).reshape(1).astype(jnp.int32)

    xs = pl.pallas_call(
        _scatter_body,
        grid=(n // TM_R,),
        in_specs=[pl.BlockSpec((TM_R * TOP_K,), lambda i: (i,), memory_space=pltpu.SMEM),
                  pl.BlockSpec((TM_R, d), lambda i: (i, 0)),
                  pl.BlockSpec(memory_space=pl.ANY)],
        out_specs=pl.BlockSpec(memory_space=pl.ANY),
        out_shape=jax.ShapeDtypeStruct((n_blocks * TM_E, d), jnp.float32),
        scratch_shapes=[pltpu.SemaphoreType.DMA(())],
        input_output_aliases={2: 0},
        compiler_params=cparams, name="moe_scatter",
    )(dest, h, jnp.zeros((n_blocks * TM_E, d), jnp.float32))

    ys = pl.pallas_call(
        _expert_body,
        grid_spec=pltpu.PrefetchScalarGridSpec(
            num_scalar_prefetch=2,
            grid=(n_blocks,),
            in_specs=[pl.BlockSpec((TM_E, d), lambda i, be, nb: (i, 0)),
                      pl.BlockSpec((1, d, 2 * D_FF), lambda i, be, nb: (be[i], 0, 0)),
                      pl.BlockSpec((1, 1, 2 * D_FF), lambda i, be, nb: (be[i], 0, 0)),
                      pl.BlockSpec((1, D_FF, d), lambda i, be, nb: (be[i], 0, 0)),
                      pl.BlockSpec((1, 1, d), lambda i, be, nb: (be[i], 0, 0))],
            out_specs=pl.BlockSpec((TM_E, d), lambda i, be, nb: (i, 0))),
        out_shape=jax.ShapeDtypeStruct((n_blocks * TM_E, d), jnp.float32),
        compiler_params=cparams, name="moe_experts",
    )(blk_e, n_used, xs, w_gu.astype(MXU_DTYPE), b_gu.reshape(N_EXPERTS, 1, 2 * D_FF),
      w_down.astype(MXU_DTYPE), b_down.reshape(N_EXPERTS, 1, d))

    return pl.pallas_call(
        _combine_body,
        grid=(n // TM_C,),
        in_specs=[pl.BlockSpec((TM_C * TOP_K,), lambda i: (i,), memory_space=pltpu.SMEM),
                  pl.BlockSpec((TM_C, LANE), lambda i: (i, 0)),
                  pl.BlockSpec((TM_C, d), lambda i: (i, 0)),
                  pl.BlockSpec((1, d), lambda i: (0, 0)),
                  pl.BlockSpec(memory_space=pl.ANY)],
        out_specs=pl.BlockSpec((TM_C, d), lambda i: (i, 0)),
        out_shape=jax.ShapeDtypeStruct((n, d), jnp.float32),
        scratch_shapes=[pltpu.VMEM((TOP_K, TM_C, d), jnp.float32), pltpu.SemaphoreType.DMA(())],
        compiler_params=cparams, name="moe_combine",
    )(dest, meta, x, ln_f.reshape(1, d), ys)


def rmsnorm(x, g):
    xf = x.astype(jnp.float32)
    y = xf * lax.rsqrt(jnp.mean(xf * xf, axis=-1, keepdims=True) + RMS_EPS)
    return (y * g.astype(jnp.float32)).astype(x.dtype)


def split_proj(h):
    out, off = {}, 0
    for name, n in SPLITS:
        out[name] = h[..., off:off + n]
        off += n
    return out


def t5_bucket(dist):
    n = jnp.maximum(dist, 0)
    max_exact = N_BUCKETS // 2
    nf = jnp.maximum(n, 1).astype(jnp.float32)
    large = max_exact + (jnp.log(nf / max_exact) / math.log(REL_MAX_DIST / max_exact)
                         * (N_BUCKETS - max_exact)).astype(jnp.int32)
    return jnp.where(n < max_exact, n, jnp.minimum(large, N_BUCKETS - 1))


def masked_softmax(s, mask):
    p = jax.nn.softmax(jnp.where(mask, s, NEG_INF), axis=-1)
    return jnp.where(mask, p, 0.0)


def cmp_to_sel_matrix(nc, ns):
    c0 = np.arange(nc) * CMP_STRIDE
    s0 = np.arange(ns) * SEL_BLOCK
    ov = np.minimum(c0[:, None] + CMP_LEN, s0[None, :] + SEL_BLOCK) - np.maximum(c0[:, None], s0[None, :])
    return jnp.asarray(np.clip(ov, 0, None) / CMP_LEN, dtype=jnp.float32)


def compress(rows, pe, w1, w2):
    B, L, G, D = rows.shape
    r = CMP_LEN // CMP_STRIDE
    nch = L // CMP_STRIDE
    nc = nch - r + 1
    ch = rows[:, :nch * CMP_STRIDE].reshape(B, nch, CMP_STRIDE, G, D)
    w1c = w1.reshape(r, CMP_STRIDE, D, CMP_HIDDEN)
    hid = jnp.einsum('ld,ldh->h', pe, w1)
    for i in range(r):
        hid = hid + jnp.einsum('bnsgd,sdh->bngh', ch[:, i:i + nc], w1c[i])
    return jax.nn.silu(hid) @ w2


def gather_pages(pool, page_table):
    nb, npg = page_table.shape
    return pool[page_table].reshape((nb, npg * pool.shape[1]) + pool.shape[2:])


def paged_rows(pool, page_table, new_rows, pos, past_len, gi=None):
    nb = pos.shape[0]
    bi = jnp.arange(nb).reshape((nb,) + (1,) * (pos.ndim - 1))
    pp = jnp.minimum(pos, past_len - 1)
    phys = page_table[bi, pp // pool.shape[1]]
    off = pp % pool.shape[1]
    npos = jnp.clip(pos - past_len, 0, new_rows.shape[1] - 1)
    if gi is None:
        past, new = pool[phys, off], new_rows[bi, npos]
    else:
        past, new = pool[phys, off, :, gi], new_rows[bi, npos, :, gi]
    is_past = (pos < past_len).reshape(pos.shape + (1,) * (past.ndim - pos.ndim))
    return jnp.where(is_past, past, new)


def nsa_attend(q, q_pos, gates, kc, vc, c_end, gather_sel, n_keys, win_rows, w_pos, rel_tab):
    B, T = q.shape[:2]
    scale = HEAD_DIM ** -0.5
    qg = q.reshape(B, T, NSA_KV, NSA_REP, HEAD_DIM)
    d_c = q_pos[:, None] - c_end[None, :]
    s = jnp.einsum('btgrd,bcgd->btgrc', qg, kc).astype(jnp.float32) * scale
    s = s + jnp.moveaxis(rel_tab[t5_bucket(d_c)], 1, -1)
    p_c = masked_softmax(s, (d_c >= 0)[:, None, None, :])
    o_c = jnp.einsum('btgrc,bcgd->btgrd', p_c.astype(vc.dtype), vc)
    nc = kc.shape[1]
    ns = -(-n_keys // SEL_BLOCK)
    blk = jnp.einsum('btgrc,cj->btgj', p_c, cmp_to_sel_matrix(nc, ns))
    j = jnp.arange(ns)
    cur = (q_pos // SEL_BLOCK)[:, None]
    visible = ((j * SEL_BLOCK)[None, :] <= q_pos[:, None])[:, None, :]
    forced = ((j[None, :] == 0) | (j[None, :] == cur) | (j[None, :] == cur - 1))[:, None, :]
    score = jnp.where(visible, jnp.where(forced, FORCED_SCORE, blk), -jnp.inf)
    _, idx = lax.top_k(score, min(N_SEL, ns))
    tok = (idx[..., None] * SEL_BLOCK + jnp.arange(SEL_BLOCK)).reshape(B, T, NSA_KV, -1)
    rows = gather_sel(jnp.minimum(tok, n_keys - 1))
    d_s = q_pos[None, :, None, None] - tok
    s = jnp.einsum('btgrd,btgkd->btgrk', qg, rows[..., 0, :]).astype(jnp.float32) * scale
    bias = rel_tab[t5_bucket(d_s), jnp.arange(NSA_KV)[:, None]]
    p_s = masked_softmax(s + jnp.moveaxis(bias, -1, 3), (d_s >= 0)[:, :, :, None, :])
    o_s = jnp.einsum('btgrk,btgkd->btgrd', p_s.astype(rows.dtype), rows[..., 1, :])
    d_w = q_pos[:, None] - w_pos[None, :]
    s = jnp.einsum('btgrd,bwgd->btgrw', qg, win_rows[:, :, 0]).astype(jnp.float32) * scale
    s = s + jnp.moveaxis(rel_tab[t5_bucket(d_w)], 1, -1)
    ok_w = (d_w >= 0) & (d_w <= WINDOW) & (w_pos >= 0)[None, :]
    p_w = masked_softmax(s, ok_w[:, None, None, :])
    o_w = jnp.einsum('btgrw,bwgd->btgrd', p_w.astype(win_rows.dtype), win_rows[:, :, 1])
    g = gates.reshape(B, T, NSA_KV, NSA_REP, 3).astype(o_c.dtype)
    o = o_c * g[..., 0:1] + o_s * g[..., 1:2] + o_w * g[..., 2:3]
    return o.reshape(B, T, NSA_HEADS * HEAD_DIM)


def dsa_attend(q, q_pos, qi, wi, ki, gather_kv, rel_tab):
    B, T = q.shape[:2]
    n_keys = ki.shape[1]
    s_i = jnp.einsum('bthe,ble->bthl', qi, ki).astype(jnp.float32) * IDX_DIM ** -0.5
    score = jnp.einsum('bthl,bth->btl', jax.nn.relu(s_i), wi.astype(jnp.float32))
    admissible = jnp.arange(n_keys)[None, :] <= q_pos[:, None]
    score = jnp.where(admissible[None], score, -jnp.inf)
    _, idx = lax.top_k(score, min(DSA_TOPK_MAX, n_keys // 4))
    rows = gather_kv(idx)
    qg = q.reshape(B, T, DSA_KV, DSA_REP, HEAD_DIM)
    d = q_pos[None, :, None] - idx
    s = jnp.einsum('btgrd,btkgd->btgrk', qg, rows[:, :, :, 0]).astype(jnp.float32) * HEAD_DIM ** -0.5
    s = s + jnp.moveaxis(rel_tab[t5_bucket(d)], 2, -1)
    p = masked_softmax(s, (d >= 0)[:, :, None, None, :])
    o = jnp.einsum('btgrk,btkgd->btgrd', p.astype(rows.dtype), rows[:, :, :, 1])
    return o.reshape(B, T, DSA_HEADS * HEAD_DIM)


def mem_attend(h, mem_kv, w_q, w_o):
    B, T = h.shape[:2]
    q = (h @ w_q).reshape(B, T, X_HEADS, X_HEAD_DIM)
    s = jnp.einsum('bthd,bmhd->bthm', q, mem_kv[:, :, 0]).astype(jnp.float32) * X_HEAD_DIM ** -0.5
    p = jax.nn.softmax(s, axis=-1)
    o = jnp.einsum('bthm,bmhd->bthd', p.astype(mem_kv.dtype), mem_kv[:, :, 1])
    return o.reshape(B, T, X_HEADS * X_HEAD_DIM) @ w_o


def layer_tail(x, o_n, o_d, merge_gate, mem_kv, w_nsa_out, w_dsa_out, w_o, ln_mem, w_mem_q, w_mem_o):
    g = jax.nn.sigmoid(merge_gate)
    x = x + (g[..., :D_MODEL] * (o_n @ w_nsa_out) + g[..., D_MODEL:] * (o_d @ w_dsa_out)) @ w_o
    return x + mem_attend(rmsnorm(x, ln_mem), mem_kv, w_mem_q, w_mem_o)


def kernel(x_prompt, x_sample, cache_nsa_cmp_kv, cache_nsa_sel_kv, cache_nsa_win_kv, cache_dsa_kv,
           cache_dsa_idx_k, cache_mem_kv, page_table, mem_prompt, ln_mix, w_in, cmp_k_pe, cmp_k_w1, cmp_k_w2,
           cmp_v_pe, cmp_v_w1, cmp_v_w2, rel_bias, w_nsa_out, w_dsa_out, w_o, ln_mem, w_mem_q, w_mem_kv,
           w_mem_o, ln_ffn, w_router, b_router, w_gu, b_gu, w_down, b_down, ln_f):
    rel_nsa = rel_bias[:, :NSA_HEADS].reshape(N_BUCKETS, NSA_KV, NSA_REP)
    rel_dsa = rel_bias[:, NSA_HEADS:].reshape(N_BUCKETS, DSA_KV, DSA_REP)
    tail_w = (w_nsa_out, w_dsa_out, w_o, ln_mem, w_mem_q, w_mem_o)

    B, L = x_prompt.shape[:2]
    pp = split_proj(norm_proj(x_prompt.reshape(B * L, D_MODEL), ln_mix, w_in).reshape(B, L, D_IN))
    kv6 = pp["nsa_kv"].reshape(B, L, 6, NSA_KV, HEAD_DIM)
    kc = compress(kv6[:, :, 0], cmp_k_pe, cmp_k_w1, cmp_k_w2)
    vc = compress(kv6[:, :, 1], cmp_v_pe, cmp_v_w1, cmp_v_w2)
    k6 = pp["nsa_kv"].reshape(B, L, 6, NSA_KV * HEAD_DIM)
    NC = kc.shape[1]
    o_n = nsa_prompt(pp["nsa_q"], pp["nsa_gate"], kc.reshape(B, NC, NSA_KV * HEAD_DIM),
                     vc.reshape(B, NC, NSA_KV * HEAD_DIM), k6[:, :, 2], k6[:, :, 3], k6[:, :, 4], k6[:, :, 5],
                     rel_bias[:, :NSA_HEADS])
    dsa_rows = pp["dsa_kv"].reshape(B, L, 2, DSA_KV, HEAD_DIM)
    dk = pp["dsa_kv"].reshape(B, L, 2, DSA_KV * HEAD_DIM)
    ki = pp["idx_k"]
    o_d = dsa_prompt(pp["dsa_q"], pp["idx_q"], pp["idx_w"], ki, dk[:, :, 0], dk[:, :, 1], rel_bias[:, NSA_HEADS:])
    mem_kv_p = (mem_prompt @ w_mem_kv).reshape(B, MEM_LEN, 2, X_HEADS, X_HEAD_DIM)
    x2_prompt = layer_tail(x_prompt, o_n, o_d, pp["merge_gate"], mem_kv_p, *tail_w)

    DB, S = x_sample.shape[:2]
    past_len = page_table.shape[1] * cache_dsa_kv.shape[1]
    n_keys = past_len + S
    q_pos = past_len + jnp.arange(S)
    ps = split_proj(norm_proj(x_sample.reshape(DB * S, D_MODEL), ln_mix, w_in).reshape(DB, S, D_IN))
    q_ns = ps["nsa_q"].reshape(DB, S, NSA_HEADS, HEAD_DIM)
    g_ns = jax.nn.sigmoid(ps["nsa_gate"]).reshape(DB, S, NSA_HEADS, 3)
    kv6s = ps["nsa_kv"].reshape(DB, S, 6, NSA_KV, HEAD_DIM)
    cmp_full = jnp.concatenate([gather_pages(cache_nsa_cmp_kv, page_table), kv6s[:, :, 0:2]], axis=1)
    kc_s = compress(cmp_full[:, :, 0], cmp_k_pe, cmp_k_w1, cmp_k_w2)
    vc_s = compress(cmp_full[:, :, 1], cmp_v_pe, cmp_v_w1, cmp_v_w2)
    c_end_s = jnp.arange(kc_s.shape[1]) * CMP_STRIDE + CMP_LEN - 1
    new_sel = kv6s[:, :, 2:4]
    gi_s = jnp.arange(NSA_KV)[None, None, :, None]
    win_rows_s = jnp.concatenate([cache_nsa_win_kv, kv6s[:, :, 4:6]], axis=1)
    wb = cache_nsa_win_kv.shape[1]
    w_pos_s = past_len - wb + jnp.arange(wb + S)
    o_ns = nsa_attend(q_ns, q_pos, g_ns, kc_s, vc_s, c_end_s,
                      lambda tok: paged_rows(cache_nsa_sel_kv, page_table, new_sel, tok, past_len, gi_s),
                      n_keys, win_rows_s, w_pos_s, rel_nsa)
    q_ds = ps["dsa_q"].reshape(DB, S, DSA_HEADS, HEAD_DIM)
    dsa_rows_s = ps["dsa_kv"].reshape(DB, S, 2, DSA_KV, HEAD_DIM)
    ki_full = jnp.concatenate([gather_pages(cache_dsa_idx_k, page_table), ps["idx_k"]], axis=1)
    o_ds = dsa_attend(q_ds, q_pos, ps["idx_q"].reshape(DB, S, IDX_HEADS, IDX_DIM), ps["idx_w"], ki_full,
                      lambda idx: paged_rows(cache_dsa_kv, page_table, dsa_rows_s, idx, past_len), rel_dsa)
    x2_sample = layer_tail(x_sample, o_ns, o_ds, ps["merge_gate"], cache_mem_kv, *tail_w)

    x2 = jnp.concatenate([x2_prompt.reshape(B * L, D_MODEL), x2_sample.reshape(DB * S, D_MODEL)], axis=0)
    y = moe_block(x2, ln_ffn, w_router, b_router, w_gu, b_gu, w_down, b_down, ln_f)
    y_prompt = y[:B * L].reshape(B, L, D_MODEL)
    y_sample = y[B * L:].reshape(DB, S, D_MODEL)
    L_w = min(WINDOW, L)
    return (y_prompt, y_sample,
            kv6[:, :, 0:2], kv6[:, :, 2:4], kv6[:, L - L_w:, 4:6], dsa_rows, ki, mem_kv_p,
            kv6s[:, :, 0:2], kv6s[:, :, 2:4], win_rows_s[:, -wb:], dsa_rows_s, ps["idx_k"])
```

```python
import math
from functools import partial

import jax
import jax.numpy as jnp
import numpy as np
from jax import lax
from jax.experimental import pallas as pl
from jax.experimental.pallas import tpu as pltpu

D_MODEL = 1024
PAGE_SIZE = 128
HEAD_DIM = 64
NSA_HEADS = 8
NSA_KV = 2
NSA_REP = NSA_HEADS // NSA_KV
CMP_LEN = 32
CMP_STRIDE = 16
CMP_HIDDEN = 128
SEL_BLOCK = 64
N_SEL = 16
WINDOW = 512
FORCED_SCORE = 1e4
DSA_HEADS = 8
DSA_KV = 2
DSA_REP = DSA_HEADS // DSA_KV
IDX_HEADS = 4
IDX_DIM = 64
DSA_TOPK_MAX = 256
MEM_LEN = 256
X_HEADS = 4
X_HEAD_DIM = 64
N_EXPERTS = 32
TOP_K = 4
D_FF = 1024
SWIGLU_LIMIT = 7.0
SWIGLU_ALPHA = 1.702
MOE_BLOCK = 128
N_BUCKETS = 32
REL_MAX_DIST = 128
Q_BLOCK = 128
RMS_EPS = 1e-6
NEG_INF = -1e30
N_REL_HEADS = NSA_HEADS + DSA_HEADS
SPLITS = (("nsa_q", NSA_HEADS * HEAD_DIM), ("nsa_kv", 6 * NSA_KV * HEAD_DIM), ("nsa_gate", 3 * NSA_HEADS),
          ("dsa_q", DSA_HEADS * HEAD_DIM), ("dsa_kv", 2 * DSA_KV * HEAD_DIM), ("idx_q", IDX_HEADS * IDX_DIM),
          ("idx_k", IDX_DIM), ("idx_w", IDX_HEADS), ("merge_gate", 2 * D_MODEL))
D_IN = sum(n for _, n in SPLITS)

LANE = 128
VMEM_LIMIT = 48 * 1024 * 1024


def _norm_proj_body(x_ref, g_ref, w_ref, o_ref):
    x = x_ref[...]
    y = x * lax.rsqrt(jnp.mean(x * x, axis=-1, keepdims=True) + RMS_EPS) * g_ref[...]
    o_ref[...] = jnp.dot(y.astype(w_ref.dtype), w_ref[...], preferred_element_type=jnp.float32)


def norm_proj(x, g, w, tm=512, tn=640):
    n, d = x.shape
    m = w.shape[1]
    m_pad = -(-m // tn) * tn
    wb = jnp.pad(w, ((0, 0), (0, m_pad - m))).astype(MXU_DTYPE)
    tm = min(tm, n)
    out = pl.pallas_call(
        _norm_proj_body,
        grid=(n // tm, m_pad // tn),
        in_specs=[pl.BlockSpec((tm, d), lambda i, j: (i, 0)),
                  pl.BlockSpec((1, d), lambda i, j: (0, 0)),
                  pl.BlockSpec((d, tn), lambda i, j: (0, j))],
        out_specs=pl.BlockSpec((tm, tn), lambda i, j: (i, j)),
        out_shape=jax.ShapeDtypeStruct((n, m_pad), jnp.float32),
        compiler_params=pltpu.CompilerParams(dimension_semantics=("arbitrary", "arbitrary"),
                                             vmem_limit_bytes=VMEM_LIMIT),
        name="norm_proj",
    )(x, g.reshape(1, d), wb)
    return out[:, :m]


MXU_DTYPE = jnp.bfloat16
TQ = 128
TK = 512
INT_MIN = -2 ** 31
M_INIT = -5e29
MAX_EXACT = N_BUCKETS // 2


def _dot_nt(a, b):
    return lax.dot_general(a, b, (((1,), (1,)), ((), ())), preferred_element_type=jnp.float32)


def _bucket_np(d):
    n = np.maximum(d, 0)
    nf = np.maximum(n, 1).astype(np.float64)
    large = MAX_EXACT + (np.log(nf / MAX_EXACT) / math.log(REL_MAX_DIST / MAX_EXACT)
                         * (N_BUCKETS - MAX_EXACT)).astype(np.int64)
    return np.where(n < MAX_EXACT, n, np.minimum(large, N_BUCKETS - 1))


def _bias_table(rel, dist):
    return jnp.take(rel.T, jnp.asarray(_bucket_np(dist), jnp.int32), axis=1)


def _flash_step(q, k, v, bias, madd, m, l, acc):
    s = _dot_nt(q, k) + bias + madd
    m_new = jnp.maximum(m, jnp.max(s, axis=1, keepdims=True))
    alpha = jnp.exp(m - m_new)
    p = jnp.exp(s - m_new)
    l = alpha * l + jnp.sum(p, axis=1, keepdims=True)
    acc = alpha * acc + jnp.dot(p.astype(v.dtype), v, preferred_element_type=jnp.float32)
    return m_new, l, acc


def _masked_branch(q, k_ref, v_ref, madd_scr, bias_last, bias_far, start, c_lo, nch):
    def body(c, carry):
        r0 = pl.multiple_of(start + c * TK, TQ)
        return _flash_step(q, k_ref[0, pl.ds(r0, TK), :], v_ref[0, pl.ds(r0, TK), :], bias_far, madd_scr[c], *carry)

    init = (jnp.full((TQ, 1), M_INIT, jnp.float32), jnp.zeros((TQ, 1), jnp.float32),
            jnp.zeros((TQ, LANE), jnp.float32))
    carry = lax.fori_loop(c_lo, nch - 1, body, init)
    r0 = pl.multiple_of(start + (nch - 1) * TK, TQ)
    m, l, acc = _flash_step(q, k_ref[0, pl.ds(r0, TK), :], v_ref[0, pl.ds(r0, TK), :], bias_last,
                            madd_scr[nch - 1], *carry)
    return acc / l


def _select_topk(key_scr, c_lo, nch, topk, n_keys):
    j_io = lax.broadcasted_iota(jnp.int32, (TQ, TK), 1)

    def count_ge(cand):
        def body(c, acc):
            x = jnp.where(key_scr[c] >= cand, 1.0, 0.0)
            for j in range(TK // LANE):
                acc = acc + x[:, LANE * j:LANE * (j + 1)]
            return acc
        acc = lax.fori_loop(c_lo, nch, body, jnp.zeros((TQ, LANE), jnp.float32))
        return jnp.sum(acc, axis=1, keepdims=True)

    def search(nbits, init):
        def body(i, cur):
            cand = cur + lax.shift_left(jnp.int32(1), nbits - 1 - i)
            return jnp.where(count_ge(cand) >= float(topk), cand, cur)
        return lax.fori_loop(0, nbits, body, init)

    thr = search(32, jnp.full((TQ, 1), INT_MIN, jnp.int32))
    nb2 = max((n_keys - 1).bit_length(), 1)

    def rekey(c, carry):
        kv = key_scr[c]
        kw = j_io + c * TK
        key_scr[c] = jnp.where(kv > thr, 2 ** nb2, jnp.where(kv == thr, (n_keys - 1) - kw, -1))
        return carry

    lax.fori_loop(c_lo, nch, rekey, 0)
    return search(nb2, jnp.zeros((TQ, 1), jnp.int32))


def _order_key(sc):
    bits = lax.bitcast_convert_type(jnp.where(sc == 0.0, 0.0, sc), jnp.int32)
    return bits ^ ((bits >> 31) & 0x7FFFFFFF)


def _dsa_body(qi_ref, wi_ref, ki_ref, q_ref, k_ref, v_ref, bias_ref, o_ref, key_scr, madd_scr, *, L, topk):
    qb = pl.program_id(1)
    start = qb * TQ
    nch = L // TK
    off = L - TQ
    c_lo = (off - start) // TK
    t_io = lax.broadcasted_iota(jnp.int32, (TQ, TK), 0)
    j_io = lax.broadcasted_iota(jnp.int32, (TQ, TK), 1)
    lane_hi = lax.broadcasted_iota(jnp.int32, (TQ, LANE), 1) >= IDX_DIM
    wi = wi_ref[0]
    qis = []
    for pr in range(IDX_HEADS // 2):
        qp = qi_ref[0, :, LANE * pr:LANE * (pr + 1)]
        qis.append(jnp.where(lane_hi, jnp.zeros_like(qp), qp))
        qis.append(jnp.where(lane_hi, qp, jnp.zeros_like(qp)))

    def valid_chunk(c):
        kpos = j_io + (c * TK + start - off)
        return (kpos >= 0) & (kpos <= t_io + start)

    def score_chunk(c, carry):
        r0 = pl.multiple_of(start + c * TK, TQ)
        kk = ki_ref[0, pl.ds(r0, TK), :]
        sc = jnp.zeros((TQ, TK), jnp.float32)
        for h in range(IDX_HEADS):
            sc = sc + jnp.maximum(_dot_nt(qis[h], kk), 0.0) * wi[:, h:h + 1]
        key_scr[c] = jnp.where(valid_chunk(c), _order_key(sc), INT_MIN)
        return carry

    lax.fori_loop(c_lo, nch, score_chunk, 0)

    thr2 = _select_topk(key_scr, c_lo, nch, topk, L)

    def finalize(c, carry):
        madd_scr[c] = jnp.where(valid_chunk(c) & (key_scr[c] >= thr2), 0.0, NEG_INF)
        return carry

    lax.fori_loop(c_lo, nch, finalize, 0)

    for h in range(DSA_HEADS):
        q = q_ref[0, :, LANE * h:LANE * (h + 1)]
        o = _masked_branch(q, k_ref, v_ref, madd_scr, bias_ref[h, :, 0:TK], bias_ref[h, :, TK:TK + 1],
                           start, c_lo, nch)
        o_ref[0, h] = o.astype(o_ref.dtype)


def _nsa_body(q_ref, gate_ref, kc_ref, vc_ref, ks_ref, vs_ref, kw_ref, vw_ref, cb_ref, mfix_ref, e3_ref,
              sbias_ref, wbias_ref, o_ref, madd_scr, *, L, WC, NS, n_sel):
    qb = pl.program_id(1)
    start = qb * TQ
    nch = L // TK
    off = L - TQ
    c_lo = (off - start) // TK
    WW = WINDOW + TQ
    t_io = lax.broadcasted_iota(jnp.int32, (TQ, TK), 0)
    j_io = lax.broadcasted_iota(jnp.int32, (TQ, TK), 1)
    gates = jax.nn.sigmoid(gate_ref[0])

    r8 = pl.multiple_of(qb * (TQ // CMP_STRIDE), 8)
    kcw = kc_ref[0, pl.ds(r8, WC), :].astype(MXU_DTYPE)
    vcw = vc_ref[0, pl.ds(r8, WC), :].astype(MXU_DTYPE)
    t_c = lax.broadcasted_iota(jnp.int32, (TQ, WC), 0)
    u_c = lax.broadcasted_iota(jnp.int32, (TQ, WC), 1)
    c_idx = u_c + (qb * (TQ // CMP_STRIDE) - (WC - TQ // CMP_STRIDE))
    d_c = t_c - CMP_STRIDE * (u_c - (WC - TQ // CMP_STRIDE)) - (CMP_LEN - 1)
    valid_c = (c_idx >= 0) & (d_c >= 0)

    t_w = lax.broadcasted_iota(jnp.int32, (TQ, WW), 0)
    j_w = lax.broadcasted_iota(jnp.int32, (TQ, WW), 1)
    d_w = t_w + WINDOW - j_w
    valid_w = (d_w >= 0) & (d_w <= WINDOW) & (j_w + (start - WINDOW) >= 0)
    madd_w = jnp.where(valid_w, 0.0, NEG_INF)
    rw = pl.multiple_of(start, TQ)
    kwin = kw_ref[0, pl.ds(rw, WW), :]
    vwin = vw_ref[0, pl.ds(rw, WW), :]

    jj = lax.broadcasted_iota(jnp.int32, (TQ, NS), 1)
    qpos = lax.broadcasted_iota(jnp.int32, (TQ, NS), 0) + start
    j_abs = jj + (qb * (TQ // SEL_BLOCK) - (NS - TQ // SEL_BLOCK))
    cur = qpos // SEL_BLOCK
    visible = (j_abs >= 0) & (j_abs * SEL_BLOCK <= qpos)
    forced = (j_abs == 0) | (j_abs == cur) | (j_abs == cur - 1)
    mfix = mfix_ref[...]

    for g in range(NSA_KV):
        psum = jnp.zeros((TQ, WC), jnp.float32)
        o_cs = []
        for r in range(NSA_REP):
            h = g * NSA_REP + r
            q = q_ref[0, :, LANE * h:LANE * (h + 1)]
            s = jnp.where(valid_c, _dot_nt(q, kcw) + cb_ref[h], NEG_INF)
            m = jnp.max(s, axis=1, keepdims=True)
            p = jnp.where(valid_c, jnp.exp(s - m), 0.0)
            l = jnp.sum(p, axis=1, keepdims=True)
            pn = p * jnp.where(l > 0.0, 1.0 / l, 0.0)
            psum = psum + pn
            o_cs.append(jnp.dot(pn.astype(MXU_DTYPE), vcw, preferred_element_type=jnp.float32))
        h1 = psum.astype(jnp.bfloat16)
        r1 = psum - h1.astype(jnp.float32)
        h2 = r1.astype(jnp.bfloat16)
        h3 = (r1 - h2.astype(jnp.float32)).astype(jnp.bfloat16)
        blk = (jnp.dot(h1, mfix, preferred_element_type=jnp.float32)
               + jnp.dot(h2, mfix, preferred_element_type=jnp.float32)
               + jnp.dot(h3, mfix, preferred_element_type=jnp.float32))
        score = jnp.where(visible, jnp.where(forced, FORCED_SCORE, blk), -jnp.inf)
        rank = jnp.zeros((TQ, NS), jnp.float32)
        for i in range(NS):
            col = score[:, i:i + 1]
            rank = rank + jnp.where((col > score) | ((col == score) & (jj > i)), 1.0, 0.0)
        sel = jnp.where(rank < float(n_sel), 1.0, 0.0).astype(jnp.bfloat16)

        def expand(c, carry):
            me = jnp.dot(sel, e3_ref[c], preferred_element_type=jnp.float32)
            kpos = j_io + (c * TK + start - off)
            ok = (me > 0.5) & (kpos >= 0) & (kpos <= t_io + start)
            madd_scr[c] = jnp.where(ok, 0.0, NEG_INF)
            return carry

        lax.fori_loop(c_lo, nch, expand, 0)

        for r in range(NSA_REP):
            h = g * NSA_REP + r
            q = q_ref[0, :, LANE * h:LANE * (h + 1)]
            o_s = _masked_branch(q, ks_ref, vs_ref, madd_scr, sbias_ref[h, :, 0:TK], sbias_ref[h, :, TK:TK + 1],
                                 start, c_lo, nch)
            s = _dot_nt(q, kwin) + wbias_ref[h] + madd_w
            p = jnp.exp(s - jnp.max(s, axis=1, keepdims=True))
            o_w = (jnp.dot(p.astype(vwin.dtype), vwin, preferred_element_type=jnp.float32)
                   / jnp.sum(p, axis=1, keepdims=True))
            o = (o_cs[r] * gates[:, 3 * h:3 * h + 1] + o_s * gates[:, 3 * h + 1:3 * h + 2]
                 + o_w * gates[:, 3 * h + 2:3 * h + 3])
            o_ref[0, h] = o.astype(o_ref.dtype)


def _pad_heads(q, n_kv):
    B, L, hd = q.shape
    rep = hd // HEAD_DIM // n_kv
    eye = jnp.eye(n_kv, dtype=q.dtype).reshape(1, 1, n_kv, 1, n_kv, 1)
    return (q.reshape(B, L, n_kv, rep, 1, HEAD_DIM) * eye).reshape(B, L, n_kv * rep * n_kv * HEAD_DIM)


def _unpad_heads(o, n_kv):
    B, H, L, _ = o.shape
    rep = H // n_kv
    o6 = o.reshape(B, n_kv, rep, L, n_kv, HEAD_DIM)
    oo = jnp.stack([o6[:, g, :, :, g, :] for g in range(n_kv)], axis=1)
    return jnp.transpose(oo, (0, 3, 1, 2, 4)).reshape(B, L, H * HEAD_DIM)


def _last_chunk_bias(rel):
    t = np.arange(TQ)[:, None]
    j = np.arange(TK)[None, :]
    last = _bias_table(rel, t + TK - TQ - j)
    far = jnp.broadcast_to(rel[N_BUCKETS - 1][:, None, None], (rel.shape[1], TQ, LANE))
    return jnp.concatenate([last, far], axis=2)


def dsa_prompt(q_d, qi, wi, ki, k_d, v_d, rel):
    B, L, _ = q_d.shape
    assert L % TK == 0 and DSA_KV * HEAD_DIM == LANE
    topk = min(DSA_TOPK_MAX, L // 4)
    nch = L // TK
    front = ((0, 0), (L - TQ, 0), (0, 0))
    qp = _pad_heads(q_d * HEAD_DIM ** -0.5, DSA_KV).astype(MXU_DTYPE)
    ki2 = jnp.pad(jnp.concatenate([ki, ki], axis=-1), front).astype(MXU_DTYPE)
    kp = jnp.pad(k_d, front).astype(MXU_DTYPE)
    vp = jnp.pad(v_d, front).astype(MXU_DTYPE)
    bias = _last_chunk_bias(rel)
    rows = 2 * L - TQ
    out = pl.pallas_call(
        partial(_dsa_body, L=L, topk=topk),
        grid=(B, L // TQ),
        in_specs=[pl.BlockSpec((1, TQ, IDX_HEADS * IDX_DIM), lambda b, i: (b, i, 0)),
                  pl.BlockSpec((1, TQ, IDX_HEADS), lambda b, i: (b, i, 0)),
                  pl.BlockSpec((1, rows, LANE), lambda b, i: (b, 0, 0)),
                  pl.BlockSpec((1, TQ, DSA_HEADS * LANE), lambda b, i: (b, i, 0)),
                  pl.BlockSpec((1, rows, LANE), lambda b, i: (b, 0, 0)),
                  pl.BlockSpec((1, rows, LANE), lambda b, i: (b, 0, 0)),
                  pl.BlockSpec((DSA_HEADS, TQ, TK + LANE), lambda b, i: (0, 0, 0))],
        out_specs=pl.BlockSpec((1, DSA_HEADS, TQ, LANE), lambda b, i: (b, 0, i, 0)),
        out_shape=jax.ShapeDtypeStruct((B, DSA_HEADS, L, LANE), jnp.float32),
        scratch_shapes=[pltpu.VMEM((nch, TQ, TK), jnp.int32), pltpu.VMEM((nch, TQ, TK), jnp.float32)],
        compiler_params=pltpu.CompilerParams(dimension_semantics=("arbitrary", "arbitrary"),
                                             vmem_limit_bytes=VMEM_LIMIT),
        name="dsa_prompt",
    )(qi.astype(MXU_DTYPE) * IDX_DIM ** -0.5, wi, ki2, qp, kp, vp, bias)
    return _unpad_heads(out, DSA_KV)


def nsa_prompt(q_n, gate, kc, vc, k_s, v_s, k_w, v_w, rel):
    B, L, _ = q_n.shape
    assert L % TK == 0 and L >= WINDOW + TQ and NSA_KV * HEAD_DIM == LANE
    nch = L // TK
    NS = L // SEL_BLOCK
    n_sel = min(N_SEL, NS)
    NC = kc.shape[1]
    cq = TQ // CMP_STRIDE
    WC = max(-(-(L // CMP_STRIDE) // LANE) * LANE, LANE)
    rows_c = WC - cq + L // CMP_STRIDE
    qp = _pad_heads(q_n * HEAD_DIM ** -0.5, NSA_KV).astype(MXU_DTYPE)
    pad_c = ((0, 0), (WC - cq, rows_c - (WC - cq) - NC), (0, 0))
    kcp, vcp = jnp.pad(kc, pad_c), jnp.pad(vc, pad_c)
    front = ((0, 0), (L - TQ, 0), (0, 0))
    ksp, vsp = jnp.pad(k_s, front).astype(MXU_DTYPE), jnp.pad(v_s, front).astype(MXU_DTYPE)
    wfront = ((0, 0), (WINDOW, 0), (0, 0))
    kwp, vwp = jnp.pad(k_w, wfront).astype(MXU_DTYPE), jnp.pad(v_w, wfront).astype(MXU_DTYPE)
    t = np.arange(TQ)[:, None]
    cb = _bias_table(rel, t - CMP_STRIDE * (np.arange(WC)[None, :] - (WC - cq)) - (CMP_LEN - 1))
    c0 = (np.arange(WC) - (WC - cq)) * CMP_STRIDE
    s0 = (np.arange(NS) - (NS - TQ // SEL_BLOCK)) * SEL_BLOCK
    ov = np.minimum(c0[:, None] + CMP_LEN, s0[None, :] + SEL_BLOCK) - np.maximum(c0[:, None], s0[None, :])
    mfix = jnp.asarray(np.clip(ov, 0, None) / CMP_LEN, jnp.bfloat16)
    e3 = (np.arange(L)[None, :] // SEL_BLOCK == np.arange(NS)[:, None]).reshape(NS, nch, TK).transpose(1, 0, 2)
    e3 = jnp.asarray(e3, jnp.bfloat16)
    sbias = _last_chunk_bias(rel)
    WW = WINDOW + TQ
    wbias = _bias_table(rel, t + WINDOW - np.arange(WW)[None, :])
    rows = 2 * L - TQ
    full = lambda shp: pl.BlockSpec(shp, lambda b, i: (0,) * len(shp))
    per_b = lambda r: pl.BlockSpec((1, r, LANE), lambda b, i: (b, 0, 0))
    out = pl.pallas_call(
        partial(_nsa_body, L=L, WC=WC, NS=NS, n_sel=n_sel),
        grid=(B, L // TQ),
        in_specs=[pl.BlockSpec((1, TQ, NSA_HEADS * LANE), lambda b, i: (b, i, 0)),
                  pl.BlockSpec((1, TQ, 3 * NSA_HEADS), lambda b, i: (b, i, 0)),
                  per_b(rows_c), per_b(rows_c), per_b(rows), per_b(rows), per_b(L + WINDOW), per_b(L + WINDOW),
                  full((NSA_HEADS, TQ, WC)), full((WC, NS)), full((nch, NS, TK)),
                  full((NSA_HEADS, TQ, TK + LANE)), full((NSA_HEADS, TQ, WW))],
        out_specs=pl.BlockSpec((1, NSA_HEADS, TQ, LANE), lambda b, i: (b, 0, i, 0)),
        out_shape=jax.ShapeDtypeStruct((B, NSA_HEADS, L, LANE), jnp.float32),
        scratch_shapes=[pltpu.VMEM((nch, TQ, TK), jnp.float32)],
        compiler_params=pltpu.CompilerParams(dimension_semantics=("arbitrary", "arbitrary"),
                                             vmem_limit_bytes=VMEM_LIMIT),
        name="nsa_prompt",
    )(qp, gate, kcp, vcp, ksp, vsp, kwp, vwp, cb, mfix, e3, sbias, wbias)
    return _unpad_heads(out, NSA_KV)


TM_R = 256
TM_E = 256
TM_C = 128


def _router_body(x_ref, g_ref, wr_ref, br_ref, h_ref, meta_ref, cnt_ref, carry_scr):
    i = pl.program_id(0)

    @pl.when(i == 0)
    def _():
        carry_scr[...] = jnp.zeros_like(carry_scr)

    x = x_ref[...]
    h = x * lax.rsqrt(jnp.mean(x * x, axis=-1, keepdims=True) + RMS_EPS) * g_ref[...]
    h_ref[...] = h
    lane = lax.broadcasted_iota(jnp.int32, (TM_R, LANE), 1)
    lane_f = lane.astype(jnp.float32)
    logits = jnp.dot(h.astype(MXU_DTYPE), wr_ref[...], preferred_element_type=jnp.float32) + br_ref[...]
    cur = jnp.where(lane < N_EXPERTS, logits, -jnp.inf)
    vals, idxs = [], []
    for _ in range(TOP_K):
        m = jnp.max(cur, axis=1, keepdims=True)
        idx = jnp.min(jnp.where(cur == m, lane_f, float(LANE)), axis=1, keepdims=True)
        vals.append(m)
        idxs.append(idx)
        cur = jnp.where(lane_f == idx, -jnp.inf, cur)
    es = [jnp.exp(v - vals[0]) for v in vals]
    denom = es[0]
    for e in es[1:]:
        denom = denom + e
    onehots = [jnp.where(lane_f == idx, 1.0, 0.0) for idx in idxs]
    osum = onehots[0]
    for o in onehots[1:]:
        osum = osum + o
    r_io = lax.broadcasted_iota(jnp.int32, (TM_R, TM_R), 0)
    c_io = lax.broadcasted_iota(jnp.int32, (TM_R, TM_R), 1)
    ltri = jnp.where(c_io < r_io, 1.0, 0.0).astype(jnp.bfloat16)
    prefix = jnp.dot(ltri, osum.astype(jnp.bfloat16), preferred_element_type=jnp.float32) + carry_scr[...]
    meta = jnp.zeros((TM_R, LANE), jnp.float32)
    for k in range(TOP_K):
        pos = jnp.sum(onehots[k] * prefix, axis=1, keepdims=True)
        meta = jnp.where(lane == k, idxs[k], meta)
        meta = jnp.where(lane == TOP_K + k, pos, meta)
        meta = jnp.where(lane == 2 * TOP_K + k, es[k] / denom, meta)
    meta_ref[...] = meta
    carry_scr[...] = carry_scr[...] + jnp.sum(osum, axis=0, keepdims=True)
    cnt_ref[...] = carry_scr[...]


def _scatter_body(dest_ref, h_ref, xs_in_ref, xs_ref, sem):
    del xs_in_ref
    n = TM_R * TOP_K

    def row_copy(j):
        return pltpu.make_async_copy(h_ref.at[pl.ds(j // TOP_K, 1), :], xs_ref.at[pl.ds(dest_ref[j], 1), :], sem)

    def issue(j, c):
        row_copy(j).start()
        return c

    def drain(j, c):
        row_copy(j).wait()
        return c

    lax.fori_loop(0, n, issue, 0)
    lax.fori_loop(0, n, drain, 0)


def _expert_body(blk_e_ref, nblk_ref, xs_ref, wgu_ref, bgu_ref, wd_ref, bd_ref, ys_ref):
    del blk_e_ref
    i = pl.program_id(0)

    @pl.when(i < nblk_ref[0])
    def _():
        gu = jnp.dot(xs_ref[...].astype(MXU_DTYPE), wgu_ref[0], preferred_element_type=jnp.float32) + bgu_ref[0]
        g = jnp.minimum(gu[:, :D_FF], SWIGLU_LIMIT)
        u = jnp.clip(gu[:, D_FF:], -SWIGLU_LIMIT, SWIGLU_LIMIT)
        hb = (u + 1.0) * (g * jax.nn.sigmoid(SWIGLU_ALPHA * g))
        ys_ref[...] = jnp.dot(hb.astype(MXU_DTYPE), wd_ref[0], preferred_element_type=jnp.float32) + bd_ref[0]

    @pl.when(i >= nblk_ref[0])
    def _():
        ys_ref[...] = jnp.zeros_like(ys_ref)


def _combine_body(dest_ref, meta_ref, x_ref, g_ref, ys_ref, y_ref, buf, sem):
    n = TM_C * TOP_K

    def row_copy(j):
        return pltpu.make_async_copy(ys_ref.at[pl.ds(dest_ref[j], 1), :],
                                     buf.at[j % TOP_K, pl.ds(j // TOP_K, 1), :], sem)

    def issue(j, c):
        row_copy(j).start()
        return c

    def drain(j, c):
        row_copy(j).wait()
        return c

    lax.fori_loop(0, n, issue, 0)
    lax.fori_loop(0, n, drain, 0)
    acc = x_ref[...]
    meta = meta_ref[...]
    for k in range(TOP_K):
        acc = acc + meta[:, 2 * TOP_K + k:2 * TOP_K + k + 1] * buf[k]
    y_ref[...] = acc * lax.rsqrt(jnp.mean(acc * acc, axis=-1, keepdims=True) + RMS_EPS) * g_ref[...]


def moe_block(x, ln_ffn, w_router, b_router, w_gu, b_gu, w_down, b_down, ln_f):
    n, d = x.shape
    assert n % TM_R == 0 and n % TM_C == 0 and N_EXPERTS <= LANE
    cparams = pltpu.CompilerParams(dimension_semantics=("arbitrary",), vmem_limit_bytes=VMEM_LIMIT)
    wr = jnp.pad(w_router, ((0, 0), (0, LANE - N_EXPERTS))).astype(MXU_DTYPE)
    br = jnp.pad(b_router, (0, LANE - N_EXPERTS)).reshape(1, LANE)
    h, meta, cnt = pl.pallas_call(
        _router_body,
        grid=(n // TM_R,),
        in_specs=[pl.BlockSpec((TM_R, d), lambda i: (i, 0)), pl.BlockSpec((1, d), lambda i: (0, 0)),
                  pl.BlockSpec((d, LANE), lambda i: (0, 0)), pl.BlockSpec((1, LANE), lambda i: (0, 0))],
        out_specs=[pl.BlockSpec((TM_R, d), lambda i: (i, 0)), pl.BlockSpec((TM_R, LANE), lambda i: (i, 0)),
                   pl.BlockSpec((1, LANE), lambda i: (0, 0))],
        out_shape=[jax.ShapeDtypeStruct((n, d), jnp.float32), jax.ShapeDtypeStruct((n, LANE), jnp.float32),
                   jax.ShapeDtypeStruct((1, LANE), jnp.float32)],
        scratch_shapes=[pltpu.VMEM((1, LANE), jnp.float32)],
        compiler_params=cparams, name="moe_router",
    )(x, ln_ffn.reshape(1, d), wr, br)

    top_e = meta[:, :TOP_K].astype(jnp.int32)
    pos = meta[:, TOP_K:2 * TOP_K].astype(jnp.int32)
    counts = cnt[0, :N_EXPERTS].astype(jnp.int32)
    padded = (counts + TM_E - 1) // TM_E * TM_E
    pad_end = jnp.cumsum(padded)
    dest = ((pad_end - padded)[top_e] + pos).reshape(-1)
    n_blocks = n * TOP_K // TM_E + N_EXPERTS
    blk_e = jnp.minimum(jnp.searchsorted(pad_end, jnp.arange(n_blocks) * TM_E, side='right'),
                        N_EXPERTS - 1).astype(jnp.int32)
    n_used = (pad_end[-1] // TM_E).reshape(1).astype(jnp.int32)

    xs = pl.pallas_call(
        _scatter_body,
        grid=(n // TM_R,),
        in_specs=[pl.BlockSpec((TM_R * TOP_K,), lambda i: (i,), memory_space=pltpu.SMEM),
                  pl.BlockSpec((TM_R, d), lambda i: (i, 0)),
                  pl.BlockSpec(memory_space=pl.ANY)],
        out_specs=pl.BlockSpec(memory_space=pl.ANY),
        out_shape=jax.ShapeDtypeStruct((n_blocks * TM_E, d), jnp.float32),
        scratch_shapes=[pltpu.SemaphoreType.DMA(())],
        input_output_aliases={2: 0},
        compiler_params=cparams, name="moe_scatter",
    )(dest, h, jnp.zeros((n_blocks * TM_E, d), jnp.float32))

    ys = pl.pallas_call(
        _expert_body,
        grid_spec=pltpu.PrefetchScalarGridSpec(
            num_scalar_prefetch=2,
            grid=(n_blocks,),
            in_specs=[pl.BlockSpec((TM_E, d), lambda i, be, nb: (i, 0)),
                      pl.BlockSpec((1, d, 2 * D_FF), lambda i, be, nb: (be[i], 0, 0)),
                      pl.BlockSpec((1, 1, 2 * D_FF), lambda i, be, nb: (be[i], 0, 0)),
                      pl.BlockSpec((1, D_FF, d), lambda i, be, nb: (be[i], 0, 0)),
                      pl.BlockSpec((1, 1, d), lambda i, be, nb: (be[i], 0, 0))],
            out_specs=pl.BlockSpec((TM_E, d), lambda i, be, nb: (i, 0))),
        out_shape=jax.ShapeDtypeStruct((n_blocks * TM_E, d), jnp.float32),
        compiler_params=cparams, name="moe_experts",
    )(blk_e, n_used, xs, w_gu.astype(MXU_DTYPE), b_gu.reshape(N_EXPERTS, 1, 2 * D_FF),
      w_down.astype(MXU_DTYPE), b_down.reshape(N_EXPERTS, 1, d))

    return pl.pallas_call(
        _combine_body,
        grid=(n // TM_C,),
        in_specs=[pl.BlockSpec((TM_C * TOP_K,), lambda i: (i,), memory_space=pltpu.SMEM),
                  pl.BlockSpec((TM_C, LANE), lambda i: (i, 0)),
                  pl.BlockSpec((TM_C, d), lambda i: (i, 0)),
                  pl.BlockSpec((1, d), lambda i: (0, 0)),
                  pl.BlockSpec(memory_space=pl.ANY)],
        out_specs=pl.BlockSpec((TM_C, d), lambda i: (i, 0)),
        out_shape=jax.ShapeDtypeStruct((n, d), jnp.float32),
        scratch_shapes=[pltpu.VMEM((TOP_K, TM_C, d), jnp.float32), pltpu.SemaphoreType.DMA(())],
        compiler_params=cparams, name="moe_combine",
    )(dest, meta, x, ln_f.reshape(1, d), ys)


PAGES_PER_STEP = TK // PAGE_SIZE
ROWS_T = 8
ROWS_H = 16


def _page_specs(feat, n_pages):
    def spec(i):
        return pl.BlockSpec((1, feat, PAGE_SIZE),
                            lambda b, P, pt: (pt[b, jnp.minimum(P * PAGES_PER_STEP + i, n_pages - 1)], 0, 0))
    return [spec(i) for i in range(PAGES_PER_STEP)]


def _chunk(pages, new_ref, is_new):
    cache = jnp.concatenate([p[0] for p in pages], axis=1)
    return jnp.where(is_new, new_ref[0], cache)


def _idx_score_body(pt_ref, qi_ref, wi_ref, new_ref, *rest):
    del pt_ref
    pages, out_ref = rest[:PAGES_PER_STEP], rest[PAGES_PER_STEP]
    is_new = pl.program_id(1) == pl.num_programs(1) - 1
    kt = _chunk(pages, new_ref, is_new).astype(MXU_DTYPE)
    wi = wi_ref[0]
    sc = jnp.zeros((ROWS_T, TK), jnp.float32)
    for h in range(IDX_HEADS):
        sc = sc + jnp.maximum(jnp.dot(qi_ref[0, h], kt, preferred_element_type=jnp.float32), 0.0) * wi[:, h:h + 1]
    out_ref[0] = sc


def _topk_rows_body(sc_ref, out_ref, key_scr, *, nch, topk, past_len):
    t_io = lax.broadcasted_iota(jnp.int32, (TQ, TK), 0)
    j_io = lax.broadcasted_iota(jnp.int32, (TQ, TK), 1)
    qpos = past_len + t_io % ROWS_T

    def valid_chunk(c):
        return j_io + c * TK <= qpos

    def load(c, carry):
        key_scr[c] = jnp.where(valid_chunk(c), _order_key(sc_ref[c]), INT_MIN)
        return carry

    lax.fori_loop(0, nch, load, 0)
    thr2 = _select_topk(key_scr, 0, nch, topk, nch * TK)

    def finalize(c, carry):
        out_ref[c] = jnp.where(valid_chunk(c) & (key_scr[c] >= thr2), 1.0, 0.0)
        return carry

    lax.fori_loop(0, nch, finalize, 0)


def _decode_attn_body(pt_ref, q_ref, selblk_ref, e3_ref, dmask_ref, bias_ref, snew_ref, dnew_ref, *rest,
                      past_len):
    del pt_ref
    n = PAGES_PER_STEP
    spages, dpages, o_ref, m_scr, l_scr, acc_scr = rest[:n], rest[n:2 * n], rest[2 * n], *rest[2 * n + 1:]
    P = pl.program_id(1)
    last = pl.num_programs(1) - 1
    is_new = P == last

    @pl.when(P == 0)
    def _():
        m_scr[...] = jnp.full_like(m_scr, M_INIT)
        l_scr[...] = jnp.zeros_like(l_scr)
        acc_scr[...] = jnp.zeros_like(acc_scr)

    row = lax.broadcasted_iota(jnp.int32, (ROWS_H, TK), 0)
    kidx = lax.broadcasted_iota(jnp.int32, (ROWS_H, TK), 1) + P * TK
    causal = kidx <= past_len + row % (ROWS_H // NSA_REP)
    rep = jnp.where(lax.broadcasted_iota(jnp.int32, (ROWS_H, ROWS_T), 0) % (ROWS_H // DSA_REP)
                    == lax.broadcasted_iota(jnp.int32, (ROWS_H, ROWS_T), 1), 1.0, 0.0).astype(jnp.bfloat16)
    dm = jnp.dot(rep, dmask_ref[0].astype(jnp.bfloat16), preferred_element_type=jnp.float32)
    bsel = jnp.where(is_new, 2, jnp.where(P == last - 1, 1, 0))
    for br, (pages, new_ref) in enumerate(((spages, snew_ref), (dpages, dnew_ref))):
        data = _chunk(pages, new_ref, is_new)
        kt = data[:LANE].astype(MXU_DTYPE)
        vt = data[LANE:].astype(MXU_DTYPE)
        for g in range(NSA_KV):
            if br == 0:
                hit = jnp.dot(selblk_ref[0, g], e3_ref[0], preferred_element_type=jnp.float32)
                valid = (hit > 0.5) & causal
            else:
                valid = dm > 0.5
            s = (jnp.dot(q_ref[0, br, g], kt, preferred_element_type=jnp.float32) + bias_ref[br, g, bsel]
                 + jnp.where(valid, 0.0, NEG_INF))
            st = br * NSA_KV + g
            m_new = jnp.maximum(m_scr[st], jnp.max(s, axis=1, keepdims=True))
            alpha = jnp.exp(m_scr[st] - m_new)
            p = jnp.exp(s - m_new)
            l_scr[st] = alpha * l_scr[st] + jnp.sum(p, axis=1, keepdims=True)
            acc_scr[st] = alpha * acc_scr[st] + _dot_nt(p.astype(MXU_DTYPE), vt)
            m_scr[st] = m_new

    @pl.when(is_new)
    def _():
        for st in range(2 * NSA_KV):
            o_ref[0, st // NSA_KV, st % NSA_KV] = acc_scr[st] / l_scr[st]


def _token_minor(cache):
    n = cache.shape[0]
    return jnp.moveaxis(cache, 1, -1).reshape(n, -1, cache.shape[1])


def _new_chunk(rows):
    db, s = rows.shape[:2]
    t = jnp.moveaxis(rows.reshape(db, s, -1), 1, 2)
    return jnp.pad(t, ((0, 0), (0, 0), (0, TK - s)))


def _rows_rt(x, n_kv):
    db, s, hd = x.shape
    rep = hd // HEAD_DIM // n_kv
    xp = _pad_heads(x, n_kv).reshape(db, s, n_kv, rep, LANE)
    return jnp.transpose(xp, (0, 2, 3, 1, 4)).reshape(db, n_kv, rep * s, LANE)


def _sample_compress_body(pt_ref, wbd_ref, pe_ref, w2_ref, *rest):
    del pt_ref
    pages, o_ref, xs_scr, acc_scr = rest[:PAGES_PER_STEP], *rest[PAGES_PER_STEP:]
    P = pl.program_id(1)
    data = jnp.concatenate([p[0] for p in pages], axis=1)
    for f in range(xs_scr.shape[0]):
        xs_scr[f, pl.ds(pl.multiple_of(P * TK, TK), TK), :] = data[f * LANE:(f + 1) * LANE].T

    @pl.when(P == pl.num_programs(1) - 1)
    def _():
        n_chunks = xs_scr.shape[1] // CMP_STRIDE
        for s in range(CMP_STRIDE):
            rows = jnp.concatenate([xs_scr[f, pl.ds(s, n_chunks, stride=CMP_STRIDE), :]
                                    for f in range(xs_scr.shape[0])], axis=1)
            part = jnp.dot(rows.astype(MXU_DTYPE), wbd_ref[s], preferred_element_type=jnp.float32)
            if s == 0:
                acc_scr[...] = part
            else:
                acc_scr[...] += part
        half = acc_scr.shape[1] // 2
        hid = acc_scr[:, :half] + pltpu.roll(acc_scr[:, half:], n_chunks - 1, 0) + pe_ref[...]
        act = hid * jax.nn.sigmoid(hid)
        o_ref[0] = jnp.dot(act.astype(MXU_DTYPE), w2_ref[...], preferred_element_type=jnp.float32)


def sample_compress(cache_cmp, page_table, k_pe, k_w1, k_w2, v_pe, v_w1, v_w2):
    DB, n_pages = page_table.shape
    assert CMP_LEN == 2 * CMP_STRIDE and n_pages % PAGES_PER_STEP == 0
    past_len = n_pages * PAGE_SIZE
    n_chunks = past_len // CMP_STRIDE
    combos = 2 * NSA_KV
    feat = combos * HEAD_DIM
    w1 = jnp.stack([k_w1] * NSA_KV + [v_w1] * NSA_KV).reshape(combos, 2, CMP_STRIDE, HEAD_DIM, CMP_HIDDEN)
    wbd = jnp.einsum('cisdh,xc->sxdich', w1, jnp.eye(combos, dtype=w1.dtype))
    wbd = wbd.reshape(CMP_STRIDE, feat, 2 * combos * CMP_HIDDEN).astype(MXU_DTYPE)
    pe = jnp.concatenate([jnp.einsum('ld,ldh->h', k_pe, k_w1)] * NSA_KV + [jnp.einsum('ld,ldh->h', v_pe, v_w1)] * NSA_KV)
    w2 = jnp.stack([k_w2] * NSA_KV + [v_w2] * NSA_KV)
    w2bd = jnp.einsum('chd,xc->chxd', w2, jnp.eye(combos, dtype=w2.dtype)).reshape(combos * CMP_HIDDEN, feat)
    return pl.pallas_call(
        _sample_compress_body,
        grid_spec=pltpu.PrefetchScalarGridSpec(
            num_scalar_prefetch=1, grid=(DB, n_pages // PAGES_PER_STEP),
            in_specs=[pl.BlockSpec((CMP_STRIDE, feat, 2 * combos * CMP_HIDDEN), lambda b, P, pt: (0, 0, 0)),
                      pl.BlockSpec((1, combos * CMP_HIDDEN), lambda b, P, pt: (0, 0)),
                      pl.BlockSpec((combos * CMP_HIDDEN, feat), lambda b, P, pt: (0, 0))]
            + _page_specs(feat, n_pages),
            out_specs=pl.BlockSpec((1, n_chunks, feat), lambda b, P, pt: (b, 0, 0)),
            scratch_shapes=[pltpu.VMEM((feat // LANE, past_len, LANE), jnp.float32),
                            pltpu.VMEM((n_chunks, 2 * combos * CMP_HIDDEN), jnp.float32)]),
        out_shape=jax.ShapeDtypeStruct((DB, n_chunks, feat), jnp.float32),
        compiler_params=pltpu.CompilerParams(dimension_semantics=("arbitrary", "arbitrary"),
                                             vmem_limit_bytes=VMEM_LIMIT),
        name="sample_compress",
    )(page_table, wbd, pe.reshape(1, -1), w2bd.astype(MXU_DTYPE), *([_token_minor(cache_cmp)] * PAGES_PER_STEP))


def sample_sparse_attention(q_n, q_d, qi, wi, ki_new, selblk, sel_new, dsa_new, cache_sel, cache_dsa, cache_idx,
                            page_table, rel_bias):
    DB, S, _ = q_n.shape
    n_pages = page_table.shape[1]
    past_len = n_pages * PAGE_SIZE
    assert n_pages % PAGES_PER_STEP == 0 and S <= ROWS_T and NSA_REP * S == ROWS_H and DSA_REP * S == ROWS_H
    nch = n_pages // PAGES_PER_STEP + 1
    topk = min(DSA_TOPK_MAX, (past_len + S) // 4)
    cparams = pltpu.CompilerParams(dimension_semantics=("arbitrary", "arbitrary"), vmem_limit_bytes=VMEM_LIMIT)

    qi8 = jnp.pad(jnp.transpose(qi.reshape(DB, S, IDX_HEADS, IDX_DIM), (0, 2, 1, 3)) * IDX_DIM ** -0.5,
                  ((0, 0), (0, 0), (0, ROWS_T - S), (0, 0))).astype(MXU_DTYPE)
    wi8 = jnp.pad(wi, ((0, 0), (0, ROWS_T - S), (0, 0)))
    scores = pl.pallas_call(
        _idx_score_body,
        grid_spec=pltpu.PrefetchScalarGridSpec(
            num_scalar_prefetch=1, grid=(DB, nch),
            in_specs=[pl.BlockSpec((1, IDX_HEADS, ROWS_T, IDX_DIM), lambda b, P, pt: (b, 0, 0, 0)),
                      pl.BlockSpec((1, ROWS_T, IDX_HEADS), lambda b, P, pt: (b, 0, 0)),
                      pl.BlockSpec((1, IDX_DIM, TK), lambda b, P, pt: (b, 0, 0))] + _page_specs(IDX_DIM, n_pages),
            out_specs=pl.BlockSpec((1, ROWS_T, TK), lambda b, P, pt: (P, b, 0))),
        out_shape=jax.ShapeDtypeStruct((nch, DB * ROWS_T, TK), jnp.float32),
        compiler_params=cparams, name="sample_idx_scores",
    )(page_table, qi8, wi8, _new_chunk(ki_new), *([_token_minor(cache_idx)] * PAGES_PER_STEP))

    rows = DB * ROWS_T
    assert rows % TQ == 0
    dmask = pl.pallas_call(
        partial(_topk_rows_body, nch=nch, topk=topk, past_len=past_len),
        grid=(rows // TQ,),
        in_specs=[pl.BlockSpec((nch, TQ, TK), lambda i: (0, i, 0))],
        out_specs=pl.BlockSpec((nch, TQ, TK), lambda i: (0, i, 0)),
        out_shape=jax.ShapeDtypeStruct((nch, rows, TK), jnp.float32),
        scratch_shapes=[pltpu.VMEM((nch, TQ, TK), jnp.int32)],
        compiler_params=pltpu.CompilerParams(dimension_semantics=("arbitrary",), vmem_limit_bytes=VMEM_LIMIT),
        name="sample_topk_mask",
    )(scores)

    q2 = jnp.stack([_rows_rt(q_n * HEAD_DIM ** -0.5, NSA_KV), _rows_rt(q_d * HEAD_DIM ** -0.5, DSA_KV)],
                   axis=1).astype(MXU_DTYPE)
    nsb = selblk.shape[-1]
    nsb_pad = -(-(nch * TK // SEL_BLOCK) // LANE) * LANE
    sb = jnp.pad(jnp.transpose(selblk, (0, 2, 1, 3)), ((0, 0), (0, 0), (0, 0), (0, nsb_pad - nsb)))
    sb = jnp.tile(sb, (1, 1, NSA_REP, 1)).astype(jnp.bfloat16)
    e3 = (np.arange(nch * TK)[None, :] // SEL_BLOCK == np.arange(nsb_pad)[:, None])
    e3 = jnp.asarray(e3.reshape(nsb_pad, nch, TK).transpose(1, 0, 2), jnp.bfloat16)
    t = np.tile(np.arange(S), ROWS_H // S)[:, None]
    j = np.arange(TK)[None, :]
    dist = np.stack([np.full((ROWS_H, TK), REL_MAX_DIST), t + TK - j, t - j])
    bias = _bias_table(rel_bias, dist)
    r_of_row = np.arange(ROWS_H) // S
    bias = jnp.stack([jnp.stack([jnp.stack([bias[br * NSA_HEADS + g * NSA_REP + r_of_row, v, np.arange(ROWS_H)]
                                            for v in range(3)]) for g in range(NSA_KV)]) for br in range(2)])
    feat = 2 * NSA_KV * HEAD_DIM
    out = pl.pallas_call(
        partial(_decode_attn_body, past_len=past_len),
        grid_spec=pltpu.PrefetchScalarGridSpec(
            num_scalar_prefetch=1, grid=(DB, nch),
            in_specs=[pl.BlockSpec((1, 2, NSA_KV, ROWS_H, LANE), lambda b, P, pt: (b, 0, 0, 0, 0)),
                      pl.BlockSpec((1, NSA_KV, ROWS_H, nsb_pad), lambda b, P, pt: (b, 0, 0, 0)),
                      pl.BlockSpec((1, nsb_pad, TK), lambda b, P, pt: (P, 0, 0)),
                      pl.BlockSpec((1, ROWS_T, TK), lambda b, P, pt: (P, b, 0)),
                      pl.BlockSpec((2, NSA_KV, 3, ROWS_H, TK), lambda b, P, pt: (0, 0, 0, 0, 0)),
                      pl.BlockSpec((1, feat, TK), lambda b, P, pt: (b, 0, 0)),
                      pl.BlockSpec((1, feat, TK), lambda b, P, pt: (b, 0, 0))]
            + _page_specs(feat, n_pages) + _page_specs(feat, n_pages),
            out_specs=pl.BlockSpec((1, 2, NSA_KV, ROWS_H, LANE), lambda b, P, pt: (b, 0, 0, 0, 0)),
            scratch_shapes=[pltpu.VMEM((2 * NSA_KV, ROWS_H, 1), jnp.float32),
                            pltpu.VMEM((2 * NSA_KV, ROWS_H, 1), jnp.float32),
                            pltpu.VMEM((2 * NSA_KV, ROWS_H, LANE), jnp.float32)]),
        out_shape=jax.ShapeDtypeStruct((DB, 2, NSA_KV, ROWS_H, LANE), jnp.float32),
        compiler_params=cparams, name="sample_decode_attn",
    )(page_table, q2, sb, e3, dmask, bias, _new_chunk(sel_new), _new_chunk(dsa_new),
      *([_token_minor(cache_sel)] * PAGES_PER_STEP), *([_token_minor(cache_dsa)] * PAGES_PER_STEP))
    o = out.reshape(DB, 2, NSA_KV, NSA_REP, S, NSA_KV, HEAD_DIM)
    o = jnp.stack([o[:, :, g, :, :, g, :] for g in range(NSA_KV)], axis=2)
    o = jnp.transpose(o, (1, 0, 4, 2, 3, 5)).reshape(2, DB, S, NSA_HEADS * HEAD_DIM)
    return o[0], o[1]


def rmsnorm(x, g):
    xf = x.astype(jnp.float32)
    y = xf * lax.rsqrt(jnp.mean(xf * xf, axis=-1, keepdims=True) + RMS_EPS)
    return (y * g.astype(jnp.float32)).astype(x.dtype)


def split_proj(h):
    out, off = {}, 0
    for name, n in SPLITS:
        out[name] = h[..., off:off + n]
        off += n
    return out


def t5_bucket(dist):
    n = jnp.maximum(dist, 0)
    max_exact = N_BUCKETS // 2
    nf = jnp.maximum(n, 1).astype(jnp.float32)
    large = max_exact + (jnp.log(nf / max_exact) / math.log(REL_MAX_DIST / max_exact)
                         * (N_BUCKETS - max_exact)).astype(jnp.int32)
    return jnp.where(n < max_exact, n, jnp.minimum(large, N_BUCKETS - 1))


def masked_softmax(s, mask):
    p = jax.nn.softmax(jnp.where(mask, s, NEG_INF), axis=-1)
    return jnp.where(mask, p, 0.0)


def cmp_to_sel_matrix(nc, ns):
    c0 = np.arange(nc) * CMP_STRIDE
    s0 = np.arange(ns) * SEL_BLOCK
    ov = np.minimum(c0[:, None] + CMP_LEN, s0[None, :] + SEL_BLOCK) - np.maximum(c0[:, None], s0[None, :])
    return jnp.asarray(np.clip(ov, 0, None) / CMP_LEN, dtype=jnp.float32)


def compress(rows, pe, w1, w2):
    B, L, G, D = rows.shape
    r = CMP_LEN // CMP_STRIDE
    nch = L // CMP_STRIDE
    nc = nch - r + 1
    ch = rows[:, :nch * CMP_STRIDE].reshape(B, nch, CMP_STRIDE, G, D)
    w1c = w1.reshape(r, CMP_STRIDE, D, CMP_HIDDEN)
    hid = jnp.einsum('ld,ldh->h', pe, w1)
    for i in range(r):
        hid = hid + jnp.einsum('bnsgd,sdh->bngh', ch[:, i:i + nc], w1c[i])
    return jax.nn.silu(hid) @ w2


def nsa_dense_branches(q, q_pos, gates, kc, vc, c_end, n_keys, win_rows, w_pos, rel_tab):
    B, T = q.shape[:2]
    scale = HEAD_DIM ** -0.5
    qg = q.reshape(B, T, NSA_KV, NSA_REP, HEAD_DIM)
    d_c = q_pos[:, None] - c_end[None, :]
    s = jnp.einsum('btgrd,bcgd->btgrc', qg, kc).astype(jnp.float32) * scale
    s = s + jnp.moveaxis(rel_tab[t5_bucket(d_c)], 1, -1)
    p_c = masked_softmax(s, (d_c >= 0)[:, None, None, :])
    o_c = jnp.einsum('btgrc,bcgd->btgrd', p_c.astype(vc.dtype), vc)
    nc = kc.shape[1]
    ns = -(-n_keys // SEL_BLOCK)
    blk = jnp.einsum('btgrc,cj->btgj', p_c, cmp_to_sel_matrix(nc, ns))
    j = jnp.arange(ns)
    cur = (q_pos // SEL_BLOCK)[:, None]
    visible = ((j * SEL_BLOCK)[None, :] <= q_pos[:, None])[:, None, :]
    forced = ((j[None, :] == 0) | (j[None, :] == cur) | (j[None, :] == cur - 1))[:, None, :]
    score = jnp.where(visible, jnp.where(forced, FORCED_SCORE, blk), -jnp.inf)
    _, idx = lax.top_k(score, min(N_SEL, ns))
    d_w = q_pos[:, None] - w_pos[None, :]
    s = jnp.einsum('btgrd,bwgd->btgrw', qg, win_rows[:, :, 0]).astype(jnp.float32) * scale
    s = s + jnp.moveaxis(rel_tab[t5_bucket(d_w)], 1, -1)
    ok_w = (d_w >= 0) & (d_w <= WINDOW) & (w_pos >= 0)[None, :]
    p_w = masked_softmax(s, ok_w[:, None, None, :])
    o_w = jnp.einsum('btgrw,bwgd->btgrd', p_w.astype(win_rows.dtype), win_rows[:, :, 1])
    g = gates.reshape(B, T, NSA_KV, NSA_REP, 3).astype(o_c.dtype)
    return o_c, o_w, g, idx


def mem_attend(h, mem_kv, w_q, w_o):
    B, T = h.shape[:2]
    q = (h @ w_q).reshape(B, T, X_HEADS, X_HEAD_DIM)
    s = jnp.einsum('bthd,bmhd->bthm', q, mem_kv[:, :, 0]).astype(jnp.float32) * X_HEAD_DIM ** -0.5
    p = jax.nn.softmax(s, axis=-1)
    o = jnp.einsum('bthm,bmhd->bthd', p.astype(mem_kv.dtype), mem_kv[:, :, 1])
    return o.reshape(B, T, X_HEADS * X_HEAD_DIM) @ w_o


def layer_tail(x, o_n, o_d, merge_gate, mem_kv, w_nsa_out, w_dsa_out, w_o, ln_mem, w_mem_q, w_mem_o):
    g = jax.nn.sigmoid(merge_gate)
    x = x + (g[..., :D_MODEL] * (o_n @ w_nsa_out) + g[..., D_MODEL:] * (o_d @ w_dsa_out)) @ w_o
    return x + mem_attend(rmsnorm(x, ln_mem), mem_kv, w_mem_q, w_mem_o)


def kernel(x_prompt, x_sample, cache_nsa_cmp_kv, cache_nsa_sel_kv, cache_nsa_win_kv, cache_dsa_kv,
           cache_dsa_idx_k, cache_mem_kv, page_table, mem_prompt, ln_mix, w_in, cmp_k_pe, cmp_k_w1, cmp_k_w2,
           cmp_v_pe, cmp_v_w1, cmp_v_w2, rel_bias, w_nsa_out, w_dsa_out, w_o, ln_mem, w_mem_q, w_mem_kv,
           w_mem_o, ln_ffn, w_router, b_router, w_gu, b_gu, w_down, b_down, ln_f):
    rel_nsa = rel_bias[:, :NSA_HEADS].reshape(N_BUCKETS, NSA_KV, NSA_REP)
    rel_dsa = rel_bias[:, NSA_HEADS:].reshape(N_BUCKETS, DSA_KV, DSA_REP)
    tail_w = (w_nsa_out, w_dsa_out, w_o, ln_mem, w_mem_q, w_mem_o)

    B, L = x_prompt.shape[:2]
    pp = split_proj(norm_proj(x_prompt.reshape(B * L, D_MODEL), ln_mix, w_in).reshape(B, L, D_IN))
    kv6 = pp["nsa_kv"].reshape(B, L, 6, NSA_KV, HEAD_DIM)
    kc = compress(kv6[:, :, 0], cmp_k_pe, cmp_k_w1, cmp_k_w2)
    vc = compress(kv6[:, :, 1], cmp_v_pe, cmp_v_w1, cmp_v_w2)
    k6 = pp["nsa_kv"].reshape(B, L, 6, NSA_KV * HEAD_DIM)
    NC = kc.shape[1]
    o_n = nsa_prompt(pp["nsa_q"], pp["nsa_gate"], kc.reshape(B, NC, NSA_KV * HEAD_DIM),
                     vc.reshape(B, NC, NSA_KV * HEAD_DIM), k6[:, :, 2], k6[:, :, 3], k6[:, :, 4], k6[:, :, 5],
                     rel_bias[:, :NSA_HEADS])
    dsa_rows = pp["dsa_kv"].reshape(B, L, 2, DSA_KV, HEAD_DIM)
    dk = pp["dsa_kv"].reshape(B, L, 2, DSA_KV * HEAD_DIM)
    ki = pp["idx_k"]
    o_d = dsa_prompt(pp["dsa_q"], pp["idx_q"], pp["idx_w"], ki, dk[:, :, 0], dk[:, :, 1], rel_bias[:, NSA_HEADS:])
    mem_kv_p = (mem_prompt @ w_mem_kv).reshape(B, MEM_LEN, 2, X_HEADS, X_HEAD_DIM)
    x2_prompt = layer_tail(x_prompt, o_n, o_d, pp["merge_gate"], mem_kv_p, *tail_w)

    DB, S = x_sample.shape[:2]
    past_len = page_table.shape[1] * cache_dsa_kv.shape[1]
    n_keys = past_len + S
    q_pos = past_len + jnp.arange(S)
    ps = split_proj(norm_proj(x_sample.reshape(DB * S, D_MODEL), ln_mix, w_in).reshape(DB, S, D_IN))
    q_ns = ps["nsa_q"].reshape(DB, S, NSA_HEADS, HEAD_DIM)
    g_ns = jax.nn.sigmoid(ps["nsa_gate"]).reshape(DB, S, NSA_HEADS, 3)
    kv6s = ps["nsa_kv"].reshape(DB, S, 6, NSA_KV, HEAD_DIM)
    assert (past_len + S) // CMP_STRIDE == past_len // CMP_STRIDE
    n_cmp = past_len // CMP_STRIDE - CMP_LEN // CMP_STRIDE + 1
    kvc = sample_compress(cache_nsa_cmp_kv, page_table, cmp_k_pe, cmp_k_w1, cmp_k_w2, cmp_v_pe, cmp_v_w1, cmp_v_w2)
    kvc = kvc[:, :n_cmp].reshape(DB, n_cmp, 2, NSA_KV, HEAD_DIM)
    kc_s, vc_s = kvc[:, :, 0], kvc[:, :, 1]
    c_end_s = jnp.arange(kc_s.shape[1]) * CMP_STRIDE + CMP_LEN - 1
    win_rows_s = jnp.concatenate([cache_nsa_win_kv, kv6s[:, :, 4:6]], axis=1)
    wb = cache_nsa_win_kv.shape[1]
    w_pos_s = past_len - wb + jnp.arange(wb + S)
    o_cs, o_ws, g_s, sel_idx = nsa_dense_branches(q_ns, q_pos, g_ns, kc_s, vc_s, c_end_s, n_keys, win_rows_s,
                                                  w_pos_s, rel_nsa)
    n_blk = -(-n_keys // SEL_BLOCK)
    selblk = jnp.sum(jax.nn.one_hot(sel_idx, n_blk, dtype=jnp.float32), axis=-2)
    dsa_rows_s = ps["dsa_kv"].reshape(DB, S, 2, DSA_KV, HEAD_DIM)
    o_ss, o_ds = sample_sparse_attention(ps["nsa_q"], ps["dsa_q"], ps["idx_q"], ps["idx_w"], ps["idx_k"], selblk,
                                         kv6s[:, :, 2:4], dsa_rows_s, cache_nsa_sel_kv, cache_dsa_kv,
                                         cache_dsa_idx_k, page_table, rel_bias)
    o_ns = (o_cs * g_s[..., 0:1] + o_ss.reshape(o_cs.shape) * g_s[..., 1:2] + o_ws * g_s[..., 2:3])
    o_ns = o_ns.reshape(DB, S, NSA_HEADS * HEAD_DIM)
    x2_sample = layer_tail(x_sample, o_ns, o_ds, ps["merge_gate"], cache_mem_kv, *tail_w)

    x2 = jnp.concatenate([x2_prompt.reshape(B * L, D_MODEL), x2_sample.reshape(DB * S, D_MODEL)], axis=0)
    y = moe_block(x2, ln_ffn, w_router, b_router, w_gu, b_gu, w_down, b_down, ln_f)
    y_prompt = y[:B * L].reshape(B, L, D_MODEL)
    y_sample = y[B * L:].reshape(DB, S, D_MODEL)
    L_w = min(WINDOW, L)
    return (y_prompt, y_sample,
            kv6[:, :, 0:2], kv6[:, :, 2:4], kv6[:, L - L_w:, 4:6], dsa_rows, ki, mem_kv_p,
            kv6s[:, :, 0:2], kv6s[:, :, 2:4], win_rows_s[:, -wb:], dsa_rows_s, ps["idx_k"])
```

```python
import math
from functools import partial

import jax
import jax.numpy as jnp
import numpy as np
from jax import lax
from jax.experimental import pallas as pl
from jax.experimental.pallas import tpu as pltpu

D_MODEL = 1024
PAGE_SIZE = 128
HEAD_DIM = 64
NSA_HEADS = 8
NSA_KV = 2
NSA_REP = NSA_HEADS // NSA_KV
CMP_LEN = 32
CMP_STRIDE = 16
CMP_HIDDEN = 128
SEL_BLOCK = 64
N_SEL = 16
WINDOW = 512
FORCED_SCORE = 1e4
DSA_HEADS = 8
DSA_KV = 2
DSA_REP = DSA_HEADS // DSA_KV
IDX_HEADS = 4
IDX_DIM = 64
DSA_TOPK_MAX = 256
MEM_LEN = 256
X_HEADS = 4
X_HEAD_DIM = 64
N_EXPERTS = 32
TOP_K = 4
D_FF = 1024
SWIGLU_LIMIT = 7.0
SWIGLU_ALPHA = 1.702
MOE_BLOCK = 128
N_BUCKETS = 32
REL_MAX_DIST = 128
Q_BLOCK = 128
RMS_EPS = 1e-6
NEG_INF = -1e30
N_REL_HEADS = NSA_HEADS + DSA_HEADS
SPLITS = (("nsa_q", NSA_HEADS * HEAD_DIM), ("nsa_kv", 6 * NSA_KV * HEAD_DIM), ("nsa_gate", 3 * NSA_HEADS),
          ("dsa_q", DSA_HEADS * HEAD_DIM), ("dsa_kv", 2 * DSA_KV * HEAD_DIM), ("idx_q", IDX_HEADS * IDX_DIM),
          ("idx_k", IDX_DIM), ("idx_w", IDX_HEADS), ("merge_gate", 2 * D_MODEL))
D_IN = sum(n for _, n in SPLITS)

LANE = 128
VMEM_LIMIT = 48 * 1024 * 1024


def _norm_proj_body(x_ref, g_ref, w_ref, o_ref):
    x = x_ref[...]
    y = x * lax.rsqrt(jnp.mean(x * x, axis=-1, keepdims=True) + RMS_EPS) * g_ref[...]
    o_ref[...] = jnp.dot(y.astype(w_ref.dtype), w_ref[...], preferred_element_type=jnp.float32)


def norm_proj(x, g, w, tm=512, tn=640):
    n, d = x.shape
    m = w.shape[1]
    m_pad = -(-m // tn) * tn
    wb = jnp.pad(w, ((0, 0), (0, m_pad - m))).astype(MXU_DTYPE)
    tm = min(tm, n)
    out = pl.pallas_call(
        _norm_proj_body,
        grid=(n // tm, m_pad // tn),
        in_specs=[pl.BlockSpec((tm, d), lambda i, j: (i, 0)),
                  pl.BlockSpec((1, d), lambda i, j: (0, 0)),
                  pl.BlockSpec((d, tn), lambda i, j: (0, j))],
        out_specs=pl.BlockSpec((tm, tn), lambda i, j: (i, j)),
        out_shape=jax.ShapeDtypeStruct((n, m_pad), jnp.float32),
        compiler_params=pltpu.CompilerParams(dimension_semantics=("arbitrary", "arbitrary"),
                                             vmem_limit_bytes=VMEM_LIMIT),
        name="norm_proj",
    )(x, g.reshape(1, d), wb)
    return out[:, :m]


MXU_DTYPE = jnp.bfloat16
TQ = 128
TK = 512
INT_MIN = -2 ** 31
M_INIT = -5e29
MAX_EXACT = N_BUCKETS // 2


def _dot_nt(a, b):
    return lax.dot_general(a, b, (((1,), (1,)), ((), ())), preferred_element_type=jnp.float32)


def _bucket_np(d):
    n = np.maximum(d, 0)
    nf = np.maximum(n, 1).astype(np.float64)
    large = MAX_EXACT + (np.log(nf / MAX_EXACT) / math.log(REL_MAX_DIST / MAX_EXACT)
                         * (N_BUCKETS - MAX_EXACT)).astype(np.int64)
    return np.where(n < MAX_EXACT, n, np.minimum(large, N_BUCKETS - 1))


def _bias_table(rel, dist):
    return jnp.take(rel.T, jnp.asarray(_bucket_np(dist), jnp.int32), axis=1)


def _flash_step(q, k, v, bias, madd, m, l, acc):
    nh = q.shape[0] // madd.shape[0]
    s = _dot_nt(q, k)
    s = (s.reshape(nh, *madd.shape) + madd[None]).reshape(s.shape) + bias
    m_new = jnp.maximum(m, jnp.max(s, axis=1, keepdims=True))
    alpha = jnp.exp(m - m_new)
    p = jnp.exp(s - m_new)
    l = alpha * l + jnp.sum(p, axis=1, keepdims=True)
    acc = alpha * acc + jnp.dot(p.astype(v.dtype), v, preferred_element_type=jnp.float32)
    return m_new, l, acc


def _masked_branch(q, k_ref, v_ref, madd_scr, bias_last, bias_far, start, c_lo, nch):
    def body(c, carry):
        r0 = pl.multiple_of(start + c * TK, TQ)
        return _flash_step(q, k_ref[0, pl.ds(r0, TK), :], v_ref[0, pl.ds(r0, TK), :], bias_far, madd_scr[c], *carry)

    rows = q.shape[0]
    init = (jnp.full((rows, 1), M_INIT, jnp.float32), jnp.zeros((rows, 1), jnp.float32),
            jnp.zeros((rows, LANE), jnp.float32))
    carry = lax.fori_loop(c_lo, nch - 1, body, init)
    r0 = pl.multiple_of(start + (nch - 1) * TK, TQ)
    m, l, acc = _flash_step(q, k_ref[0, pl.ds(r0, TK), :], v_ref[0, pl.ds(r0, TK), :], bias_last,
                            madd_scr[nch - 1], *carry)
    return acc / l


def _group_rows(ref, g, rep, cols=None):
    blk = ref[g * rep:(g + 1) * rep] if cols is None else ref[g * rep:(g + 1) * rep, :, cols]
    return blk.reshape(rep * TQ, blk.shape[-1])


def _group_q(q_ref, g, rep):
    return jnp.concatenate([q_ref[0, :, LANE * h:LANE * (h + 1)] for h in range(g * rep, (g + 1) * rep)], axis=0)


def _select_topk(key_scr, c_lo, nch, topk, n_keys):
    j_io = lax.broadcasted_iota(jnp.int32, (TQ, TK), 1)

    def count_ge(cand):
        def body(c, acc):
            x = jnp.where(key_scr[c] >= cand, 1.0, 0.0)
            for j in range(TK // LANE):
                acc = acc + x[:, LANE * j:LANE * (j + 1)]
            return acc
        acc = lax.fori_loop(c_lo, nch, body, jnp.zeros((TQ, LANE), jnp.float32))
        return jnp.sum(acc, axis=1, keepdims=True)

    def search(nbits, init):
        def body(i, cur):
            cand = cur + lax.shift_left(jnp.int32(1), nbits - 1 - i)
            return jnp.where(count_ge(cand) >= float(topk), cand, cur)
        return lax.fori_loop(0, nbits, body, init)

    thr = search(32, jnp.full((TQ, 1), INT_MIN, jnp.int32))
    any_ties = jnp.max(count_ge(thr)) > float(topk)
    nb2 = max((n_keys - 1).bit_length(), 1)

    def rekey(c, carry):
        kv = key_scr[c]
        kw = j_io + c * TK
        key_scr[c] = jnp.where(kv > thr, 2 ** nb2, jnp.where(kv == thr, (n_keys - 1) - kw, -1))
        return carry

    lax.fori_loop(c_lo, nch, rekey, 0)
    zero = jnp.zeros((TQ, 1), jnp.int32)
    return lax.cond(any_ties, lambda: search(nb2, zero), lambda: zero)


def _order_key(sc):
    bits = lax.bitcast_convert_type(jnp.where(sc == 0.0, 0.0, sc), jnp.int32)
    return bits ^ ((bits >> 31) & 0x7FFFFFFF)


def _dsa_body(qi_ref, wi_ref, ki_ref, q_ref, k_ref, v_ref, bias_ref, o_ref, key_scr, madd_scr, *, L, topk):
    qb = pl.program_id(1)
    start = qb * TQ
    nch = L // TK
    off = L - TQ
    c_lo = (off - start) // TK
    t_io = lax.broadcasted_iota(jnp.int32, (TQ, TK), 0)
    j_io = lax.broadcasted_iota(jnp.int32, (TQ, TK), 1)
    lane_hi = lax.broadcasted_iota(jnp.int32, (TQ, LANE), 1) >= IDX_DIM
    wi = wi_ref[0]
    qis = []
    for pr in range(IDX_HEADS // 2):
        qp = qi_ref[0, :, LANE * pr:LANE * (pr + 1)]
        qis.append(jnp.where(lane_hi, jnp.zeros_like(qp), qp))
        qis.append(jnp.where(lane_hi, qp, jnp.zeros_like(qp)))

    def valid_chunk(c):
        kpos = j_io + (c * TK + start - off)
        return (kpos >= 0) & (kpos <= t_io + start)

    def score_chunk(c, carry):
        r0 = pl.multiple_of(start + c * TK, TQ)
        kk = ki_ref[0, pl.ds(r0, TK), :]
        sc = jnp.zeros((TQ, TK), jnp.float32)
        for h in range(IDX_HEADS):
            sc = sc + jnp.maximum(_dot_nt(qis[h], kk), 0.0) * wi[:, h:h + 1]
        key_scr[c] = jnp.where(valid_chunk(c), _order_key(sc), INT_MIN)
        return carry

    lax.fori_loop(c_lo, nch, score_chunk, 0)

    thr2 = _select_topk(key_scr, c_lo, nch, topk, L)

    def finalize(c, carry):
        madd_scr[c] = jnp.where(valid_chunk(c) & (key_scr[c] >= thr2), 0.0, NEG_INF)
        return carry

    lax.fori_loop(c_lo, nch, finalize, 0)

    for g in range(DSA_KV):
        o = _masked_branch(_group_q(q_ref, g, DSA_REP), k_ref, v_ref, madd_scr,
                           _group_rows(bias_ref, g, DSA_REP, slice(0, TK)),
                           _group_rows(bias_ref, g, DSA_REP, slice(TK, TK + 1)), start, c_lo, nch)
        for r in range(DSA_REP):
            o_ref[0, g * DSA_REP + r] = o[r * TQ:(r + 1) * TQ].astype(o_ref.dtype)


def _nsa_body(q_ref, gate_ref, kc_ref, vc_ref, ks_ref, vs_ref, kw_ref, vw_ref, cb_ref, mfix_ref, e3_ref,
              sbias_ref, wbias_ref, o_ref, madd_scr, *, L, WC, NS, n_sel):
    qb = pl.program_id(1)
    start = qb * TQ
    nch = L // TK
    off = L - TQ
    c_lo = (off - start) // TK
    WW = WINDOW + TQ
    t_io = lax.broadcasted_iota(jnp.int32, (TQ, TK), 0)
    j_io = lax.broadcasted_iota(jnp.int32, (TQ, TK), 1)
    gates = jax.nn.sigmoid(gate_ref[0])

    r8 = pl.multiple_of(qb * (TQ // CMP_STRIDE), 8)
    kcw = kc_ref[0, pl.ds(r8, WC), :].astype(MXU_DTYPE)
    vcw = vc_ref[0, pl.ds(r8, WC), :].astype(MXU_DTYPE)
    t_c = lax.broadcasted_iota(jnp.int32, (TQ, WC), 0)
    u_c = lax.broadcasted_iota(jnp.int32, (TQ, WC), 1)
    c_idx = u_c + (qb * (TQ // CMP_STRIDE) - (WC - TQ // CMP_STRIDE))
    d_c = t_c - CMP_STRIDE * (u_c - (WC - TQ // CMP_STRIDE)) - (CMP_LEN - 1)
    valid_c = (c_idx >= 0) & (d_c >= 0)

    t_w = lax.broadcasted_iota(jnp.int32, (TQ, WW), 0)
    j_w = lax.broadcasted_iota(jnp.int32, (TQ, WW), 1)
    d_w = t_w + WINDOW - j_w
    valid_w = (d_w >= 0) & (d_w <= WINDOW) & (j_w + (start - WINDOW) >= 0)
    madd_w = jnp.where(valid_w, 0.0, NEG_INF)
    rw = pl.multiple_of(start, TQ)
    kwin = kw_ref[0, pl.ds(rw, WW), :]
    vwin = vw_ref[0, pl.ds(rw, WW), :]

    jj = lax.broadcasted_iota(jnp.int32, (TQ, NS), 1)
    qpos = lax.broadcasted_iota(jnp.int32, (TQ, NS), 0) + start
    j_abs = jj + (qb * (TQ // SEL_BLOCK) - (NS - TQ // SEL_BLOCK))
    cur = qpos // SEL_BLOCK
    visible = (j_abs >= 0) & (j_abs * SEL_BLOCK <= qpos)
    forced = (j_abs == 0) | (j_abs == cur) | (j_abs == cur - 1)
    mfix = mfix_ref[...]

    for g in range(NSA_KV):
        psum = jnp.zeros((TQ, WC), jnp.float32)
        o_cs = []
        for r in range(NSA_REP):
            h = g * NSA_REP + r
            q = q_ref[0, :, LANE * h:LANE * (h + 1)]
            s = jnp.where(valid_c, _dot_nt(q, kcw) + cb_ref[h], NEG_INF)
            m = jnp.max(s, axis=1, keepdims=True)
            p = jnp.where(valid_c, jnp.exp(s - m), 0.0)
            l = jnp.sum(p, axis=1, keepdims=True)
            pn = p * jnp.where(l > 0.0, 1.0 / l, 0.0)
            psum = psum + pn
            o_cs.append(jnp.dot(pn.astype(MXU_DTYPE), vcw, preferred_element_type=jnp.float32))
        h1 = psum.astype(jnp.bfloat16)
        r1 = psum - h1.astype(jnp.float32)
        h2 = r1.astype(jnp.bfloat16)
        h3 = (r1 - h2.astype(jnp.float32)).astype(jnp.bfloat16)
        blk = (jnp.dot(h1, mfix, preferred_element_type=jnp.float32)
               + jnp.dot(h2, mfix, preferred_element_type=jnp.float32)
               + jnp.dot(h3, mfix, preferred_element_type=jnp.float32))
        score = jnp.where(visible, jnp.where(forced, FORCED_SCORE, blk), -jnp.inf)
        rank = jnp.zeros((TQ, NS), jnp.float32)
        for i in range(NS):
            col = score[:, i:i + 1]
            rank = rank + jnp.where((col > score) | ((col == score) & (jj > i)), 1.0, 0.0)
        sel = jnp.where(rank < float(n_sel), 1.0, 0.0).astype(jnp.bfloat16)

        def expand(c, carry):
            me = jnp.dot(sel, e3_ref[c], preferred_element_type=jnp.float32)
            kpos = j_io + (c * TK + start - off)
            ok = (me > 0.5) & (kpos >= 0) & (kpos <= t_io + start)
            madd_scr[c] = jnp.where(ok, 0.0, NEG_INF)
            return carry

        lax.fori_loop(c_lo, nch, expand, 0)

        q4 = _group_q(q_ref, g, NSA_REP)
        o_s = _masked_branch(q4, ks_ref, vs_ref, madd_scr, _group_rows(sbias_ref, g, NSA_REP, slice(0, TK)),
                             _group_rows(sbias_ref, g, NSA_REP, slice(TK, TK + 1)), start, c_lo, nch)
        s = _dot_nt(q4, kwin)
        s = (s.reshape(NSA_REP, TQ, WW) + madd_w[None]).reshape(s.shape) + _group_rows(wbias_ref, g, NSA_REP)
        p = jnp.exp(s - jnp.max(s, axis=1, keepdims=True))
        o_w = (jnp.dot(p.astype(vwin.dtype), vwin, preferred_element_type=jnp.float32)
               / jnp.sum(p, axis=1, keepdims=True))
        for r in range(NSA_REP):
            h = g * NSA_REP + r
            rows = slice(r * TQ, (r + 1) * TQ)
            o = (o_cs[r] * gates[:, 3 * h:3 * h + 1] + o_s[rows] * gates[:, 3 * h + 1:3 * h + 2]
                 + o_w[rows] * gates[:, 3 * h + 2:3 * h + 3])
            o_ref[0, h] = o.astype(o_ref.dtype)


def _pad_heads(q, n_kv):
    B, L, hd = q.shape
    rep = hd // HEAD_DIM // n_kv
    eye = jnp.eye(n_kv, dtype=q.dtype).reshape(1, 1, n_kv, 1, n_kv, 1)
    return (q.reshape(B, L, n_kv, rep, 1, HEAD_DIM) * eye).reshape(B, L, n_kv * rep * n_kv * HEAD_DIM)


def _unpad_heads(o, n_kv):
    B, H, L, _ = o.shape
    rep = H // n_kv
    o6 = o.reshape(B, n_kv, rep, L, n_kv, HEAD_DIM)
    oo = jnp.stack([o6[:, g, :, :, g, :] for g in range(n_kv)], axis=1)
    return jnp.transpose(oo, (0, 3, 1, 2, 4)).reshape(B, L, H * HEAD_DIM)


def _last_chunk_bias(rel):
    t = np.arange(TQ)[:, None]
    j = np.arange(TK)[None, :]
    last = _bias_table(rel, t + TK - TQ - j)
    far = jnp.broadcast_to(rel[N_BUCKETS - 1][:, None, None], (rel.shape[1], TQ, LANE))
    return jnp.concatenate([last, far], axis=2)


def dsa_prompt(q_d, qi, wi, ki, k_d, v_d, rel):
    B, L, _ = q_d.shape
    assert L % TK == 0 and DSA_KV * HEAD_DIM == LANE
    topk = min(DSA_TOPK_MAX, L // 4)
    nch = L // TK
    front = ((0, 0), (L - TQ, 0), (0, 0))
    qp = _pad_heads(q_d * HEAD_DIM ** -0.5, DSA_KV).astype(MXU_DTYPE)
    ki2 = jnp.pad(jnp.concatenate([ki, ki], axis=-1), front).astype(MXU_DTYPE)
    kp = jnp.pad(k_d, front).astype(MXU_DTYPE)
    vp = jnp.pad(v_d, front).astype(MXU_DTYPE)
    bias = _last_chunk_bias(rel)
    rows = 2 * L - TQ
    out = pl.pallas_call(
        partial(_dsa_body, L=L, topk=topk),
        grid=(B, L // TQ),
        in_specs=[pl.BlockSpec((1, TQ, IDX_HEADS * IDX_DIM), lambda b, i: (b, i, 0)),
                  pl.BlockSpec((1, TQ, IDX_HEADS), lambda b, i: (b, i, 0)),
                  pl.BlockSpec((1, rows, LANE), lambda b, i: (b, 0, 0)),
                  pl.BlockSpec((1, TQ, DSA_HEADS * LANE), lambda b, i: (b, i, 0)),
                  pl.BlockSpec((1, rows, LANE), lambda b, i: (b, 0, 0)),
                  pl.BlockSpec((1, rows, LANE), lambda b, i: (b, 0, 0)),
                  pl.BlockSpec((DSA_HEADS, TQ, TK + LANE), lambda b, i: (0, 0, 0))],
        out_specs=pl.BlockSpec((1, DSA_HEADS, TQ, LANE), lambda b, i: (b, 0, i, 0)),
        out_shape=jax.ShapeDtypeStruct((B, DSA_HEADS, L, LANE), jnp.float32),
        scratch_shapes=[pltpu.VMEM((nch, TQ, TK), jnp.int32), pltpu.VMEM((nch, TQ, TK), jnp.float32)],
        compiler_params=pltpu.CompilerParams(dimension_semantics=("arbitrary", "arbitrary"),
                                             vmem_limit_bytes=VMEM_LIMIT),
        name="dsa_prompt",
    )(qi.astype(MXU_DTYPE) * IDX_DIM ** -0.5, wi, ki2, qp, kp, vp, bias)
    return _unpad_heads(out, DSA_KV)


def nsa_prompt(q_n, gate, kc, vc, k_s, v_s, k_w, v_w, rel):
    B, L, _ = q_n.shape
    assert L % TK == 0 and L >= WINDOW + TQ and NSA_KV * HEAD_DIM == LANE
    nch = L // TK
    NS = L // SEL_BLOCK
    n_sel = min(N_SEL, NS)
    NC = kc.shape[1]
    cq = TQ // CMP_STRIDE
    WC = max(-(-(L // CMP_STRIDE) // LANE) * LANE, LANE)
    rows_c = WC - cq + L // CMP_STRIDE
    qp = _pad_heads(q_n * HEAD_DIM ** -0.5, NSA_KV).astype(MXU_DTYPE)
    pad_c = ((0, 0), (WC - cq, rows_c - (WC - cq) - NC), (0, 0))
    kcp, vcp = jnp.pad(kc, pad_c), jnp.pad(vc, pad_c)
    front = ((0, 0), (L - TQ, 0), (0, 0))
    ksp, vsp = jnp.pad(k_s, front).astype(MXU_DTYPE), jnp.pad(v_s, front).astype(MXU_DTYPE)
    wfront = ((0, 0), (WINDOW, 0), (0, 0))
    kwp, vwp = jnp.pad(k_w, wfront).astype(MXU_DTYPE), jnp.pad(v_w, wfront).astype(MXU_DTYPE)
    t = np.arange(TQ)[:, None]
    cb = _bias_table(rel, t - CMP_STRIDE * (np.arange(WC)[None, :] - (WC - cq)) - (CMP_LEN - 1))
    c0 = (np.arange(WC) - (WC - cq)) * CMP_STRIDE
    s0 = (np.arange(NS) - (NS - TQ // SEL_BLOCK)) * SEL_BLOCK
    ov = np.minimum(c0[:, None] + CMP_LEN, s0[None, :] + SEL_BLOCK) - np.maximum(c0[:, None], s0[None, :])
    mfix = jnp.asarray(np.clip(ov, 0, None) / CMP_LEN, jnp.bfloat16)
    e3 = (np.arange(L)[None, :] // SEL_BLOCK == np.arange(NS)[:, None]).reshape(NS, nch, TK).transpose(1, 0, 2)
    e3 = jnp.asarray(e3, jnp.bfloat16)
    sbias = _last_chunk_bias(rel)
    WW = WINDOW + TQ
    wbias = _bias_table(rel, t + WINDOW - np.arange(WW)[None, :])
    rows = 2 * L - TQ
    full = lambda shp: pl.BlockSpec(shp, lambda b, i: (0,) * len(shp))
    per_b = lambda r: pl.BlockSpec((1, r, LANE), lambda b, i: (b, 0, 0))
    out = pl.pallas_call(
        partial(_nsa_body, L=L, WC=WC, NS=NS, n_sel=n_sel),
        grid=(B, L // TQ),
        in_specs=[pl.BlockSpec((1, TQ, NSA_HEADS * LANE), lambda b, i: (b, i, 0)),
                  pl.BlockSpec((1, TQ, 3 * NSA_HEADS), lambda b, i: (b, i, 0)),
                  per_b(rows_c), per_b(rows_c), per_b(rows), per_b(rows), per_b(L + WINDOW), per_b(L + WINDOW),
                  full((NSA_HEADS, TQ, WC)), full((WC, NS)), full((nch, NS, TK)),
                  full((NSA_HEADS, TQ, TK + LANE)), full((NSA_HEADS, TQ, WW))],
        out_specs=pl.BlockSpec((1, NSA_HEADS, TQ, LANE), lambda b, i: (b, 0, i, 0)),
        out_shape=jax.ShapeDtypeStruct((B, NSA_HEADS, L, LANE), jnp.float32),
        scratch_shapes=[pltpu.VMEM((nch, TQ, TK), jnp.float32)],
        compiler_params=pltpu.CompilerParams(dimension_semantics=("arbitrary", "arbitrary"),
                                             vmem_limit_bytes=VMEM_LIMIT),
        name="nsa_prompt",
    )(qp, gate, kcp, vcp, ksp, vsp, kwp, vwp, cb, mfix, e3, sbias, wbias)
    return _unpad_heads(out, NSA_KV)


TM_R = 256
TM_E = 256
TM_C = 128
DMA_UNROLL = 8


def _router_body(x_ref, g_ref, wr_ref, br_ref, h_ref, meta_ref, cnt_ref, carry_scr):
    i = pl.program_id(0)

    @pl.when(i == 0)
    def _():
        carry_scr[...] = jnp.zeros_like(carry_scr)

    x = x_ref[...]
    h = x * lax.rsqrt(jnp.mean(x * x, axis=-1, keepdims=True) + RMS_EPS) * g_ref[...]
    h_ref[...] = h
    lane = lax.broadcasted_iota(jnp.int32, (TM_R, LANE), 1)
    lane_f = lane.astype(jnp.float32)
    logits = jnp.dot(h.astype(MXU_DTYPE), wr_ref[...], preferred_element_type=jnp.float32) + br_ref[...]
    cur = jnp.where(lane < N_EXPERTS, logits, -jnp.inf)
    vals, idxs = [], []
    for _ in range(TOP_K):
        m = jnp.max(cur, axis=1, keepdims=True)
        idx = jnp.min(jnp.where(cur == m, lane_f, float(LANE)), axis=1, keepdims=True)
        vals.append(m)
        idxs.append(idx)
        cur = jnp.where(lane_f == idx, -jnp.inf, cur)
    es = [jnp.exp(v - vals[0]) for v in vals]
    denom = es[0]
    for e in es[1:]:
        denom = denom + e
    onehots = [jnp.where(lane_f == idx, 1.0, 0.0) for idx in idxs]
    osum = onehots[0]
    for o in onehots[1:]:
        osum = osum + o
    r_io = lax.broadcasted_iota(jnp.int32, (TM_R, TM_R), 0)
    c_io = lax.broadcasted_iota(jnp.int32, (TM_R, TM_R), 1)
    ltri = jnp.where(c_io < r_io, 1.0, 0.0).astype(jnp.bfloat16)
    prefix = jnp.dot(ltri, osum.astype(jnp.bfloat16), preferred_element_type=jnp.float32) + carry_scr[...]
    meta = jnp.zeros((TM_R, LANE), jnp.float32)
    for k in range(TOP_K):
        pos = jnp.sum(onehots[k] * prefix, axis=1, keepdims=True)
        meta = jnp.where(lane == k, idxs[k], meta)
        meta = jnp.where(lane == TOP_K + k, pos, meta)
        meta = jnp.where(lane == 2 * TOP_K + k, es[k] / denom, meta)
    meta_ref[...] = meta
    carry_scr[...] = carry_scr[...] + jnp.sum(osum, axis=0, keepdims=True)
    cnt_ref[...] = carry_scr[...]


def _scatter_body(dest_ref, h_ref, xs_in_ref, xs_ref, sem):
    del xs_in_ref
    n = TM_R * TOP_K

    def row_copy(j):
        return pltpu.make_async_copy(h_ref.at[pl.ds(j // TOP_K, 1), :], xs_ref.at[pl.ds(dest_ref[j], 1), :], sem)

    def issue(j, c):
        row_copy(j).start()
        return c

    def drain(j, c):
        row_copy(j).wait()
        return c

    lax.fori_loop(0, n, issue, 0, unroll=DMA_UNROLL)
    lax.fori_loop(0, n, drain, 0, unroll=DMA_UNROLL)


def _expert_body(blk_e_ref, nblk_ref, xs_ref, wgu_ref, bgu_ref, wd_ref, bd_ref, ys_ref):
    del blk_e_ref
    i = pl.program_id(0)

    @pl.when(i < nblk_ref[0])
    def _():
        gu = jnp.dot(xs_ref[...].astype(MXU_DTYPE), wgu_ref[0], preferred_element_type=jnp.float32) + bgu_ref[0]
        g = jnp.minimum(gu[:, :D_FF], SWIGLU_LIMIT)
        u = jnp.clip(gu[:, D_FF:], -SWIGLU_LIMIT, SWIGLU_LIMIT)
        hb = (u + 1.0) * (g * jax.nn.sigmoid(SWIGLU_ALPHA * g))
        ys_ref[...] = jnp.dot(hb.astype(MXU_DTYPE), wd_ref[0], preferred_element_type=jnp.float32) + bd_ref[0]

    @pl.when(i >= nblk_ref[0])
    def _():
        ys_ref[...] = jnp.zeros_like(ys_ref)


def _combine_body(dest_ref, meta_ref, x_ref, g_ref, ys_ref, y_ref, buf, sem):
    n = TM_C * TOP_K

    def row_copy(j):
        return pltpu.make_async_copy(ys_ref.at[pl.ds(dest_ref[j], 1), :],
                                     buf.at[j % TOP_K, pl.ds(j // TOP_K, 1), :], sem)

    def issue(j, c):
        row_copy(j).start()
        return c

    def drain(j, c):
        row_copy(j).wait()
        return c

    lax.fori_loop(0, n, issue, 0, unroll=DMA_UNROLL)
    lax.fori_loop(0, n, drain, 0, unroll=DMA_UNROLL)
    acc = x_ref[...]
    meta = meta_ref[...]
    for k in range(TOP_K):
        acc = acc + meta[:, 2 * TOP_K + k:2 * TOP_K + k + 1] * buf[k]
    y_ref[...] = acc * lax.rsqrt(jnp.mean(acc * acc, axis=-1, keepdims=True) + RMS_EPS) * g_ref[...]


def moe_block(x, ln_ffn, w_router, b_router, w_gu, b_gu, w_down, b_down, ln_f):
    n, d = x.shape
    assert n % TM_R == 0 and n % TM_C == 0 and N_EXPERTS <= LANE
    cparams = pltpu.CompilerParams(dimension_semantics=("arbitrary",), vmem_limit_bytes=VMEM_LIMIT)
    wr = jnp.pad(w_router, ((0, 0), (0, LANE - N_EXPERTS))).astype(MXU_DTYPE)
    br = jnp.pad(b_router, (0, LANE - N_EXPERTS)).reshape(1, LANE)
    h, meta, cnt = pl.pallas_call(
        _router_body,
        grid=(n // TM_R,),
        in_specs=[pl.BlockSpec((TM_R, d), lambda i: (i, 0)), pl.BlockSpec((1, d), lambda i: (0, 0)),
                  pl.BlockSpec((d, LANE), lambda i: (0, 0)), pl.BlockSpec((1, LANE), lambda i: (0, 0))],
        out_specs=[pl.BlockSpec((TM_R, d), lambda i: (i, 0)), pl.BlockSpec((TM_R, LANE), lambda i: (i, 0)),
                   pl.BlockSpec((1, LANE), lambda i: (0, 0))],
        out_shape=[jax.ShapeDtypeStruct((n, d), jnp.float32), jax.ShapeDtypeStruct((n, LANE), jnp.float32),
                   jax.ShapeDtypeStruct((1, LANE), jnp.float32)],
        scratch_shapes=[pltpu.VMEM((1, LANE), jnp.float32)],
        compiler_params=cparams, name="moe_router",
    )(x, ln_ffn.reshape(1, d), wr, br)

    top_e = meta[:, :TOP_K].astype(jnp.int32)
    pos = meta[:, TOP_K:2 * TOP_K].astype(jnp.int32)
    counts = cnt[0, :N_EXPERTS].astype(jnp.int32)
    padded = (counts + TM_E - 1) // TM_E * TM_E
    pad_end = jnp.cumsum(padded)
    dest = ((pad_end - padded)[top_e] + pos).reshape(-1)
    n_blocks = n * TOP_K // TM_E + N_EXPERTS
    blk_e = jnp.minimum(jnp.searchsorted(pad_end, jnp.arange(n_blocks) * TM_E, side='right'),
                        N_EXPERTS - 1).astype(jnp.int32)
    n_used = (pad_end[-1] // TM_E).reshape(1).astype(jnp.int32)

    xs = pl.pallas_call(
        _scatter_body,
        grid=(n // TM_R,),
        in_specs=[pl.BlockSpec((TM_R * TOP_K,), lambda i: (i,), memory_space=pltpu.SMEM),
                  pl.BlockSpec((TM_R, d), lambda i: (i, 0)),
                  pl.BlockSpec(memory_space=pl.ANY)],
        out_specs=pl.BlockSpec(memory_space=pl.ANY),
        out_shape=jax.ShapeDtypeStruct((n_blocks * TM_E, d), jnp.float32),
        scratch_shapes=[pltpu.SemaphoreType.DMA(())],
        input_output_aliases={2: 0},
        compiler_params=cparams, name="moe_scatter",
    )(dest, h, jnp.zeros((n_blocks * TM_E, d), jnp.float32))

    ys = pl.pallas_call(
        _expert_body,
        grid_spec=pltpu.PrefetchScalarGridSpec(
            num_scalar_prefetch=2,
            grid=(n_blocks,),
            in_specs=[pl.BlockSpec((TM_E, d), lambda i, be, nb: (i, 0)),
                      pl.BlockSpec((1, d, 2 * D_FF), lambda i, be, nb: (be[i], 0, 0)),
                      pl.BlockSpec((1, 1, 2 * D_FF), lambda i, be, nb: (be[i], 0, 0)),
                      pl.BlockSpec((1, D_FF, d), lambda i, be, nb: (be[i], 0, 0)),
                      pl.BlockSpec((1, 1, d), lambda i, be, nb: (be[i], 0, 0))],
            out_specs=pl.BlockSpec((TM_E, d), lambda i, be, nb: (i, 0))),
        out_shape=jax.ShapeDtypeStruct((n_blocks * TM_E, d), jnp.float32),
        compiler_params=cparams, name="moe_experts",
    )(blk_e, n_used, xs, w_gu.astype(MXU_DTYPE), b_gu.reshape(N_EXPERTS, 1, 2 * D_FF),
      w_down.astype(MXU_DTYPE), b_down.reshape(N_EXPERTS, 1, d))

    return pl.pallas_call(
        _combine_body,
        grid=(n // TM_C,),
        in_specs=[pl.BlockSpec((TM_C * TOP_K,), lambda i: (i,), memory_space=pltpu.SMEM),
                  pl.BlockSpec((TM_C, LANE), lambda i: (i, 0)),
                  pl.BlockSpec((TM_C, d), lambda i: (i, 0)),
                  pl.BlockSpec((1, d), lambda i: (0, 0)),
                  pl.BlockSpec(memory_space=pl.ANY)],
        out_specs=pl.BlockSpec((TM_C, d), lambda i: (i, 0)),
        out_shape=jax.ShapeDtypeStruct((n, d), jnp.float32),
        scratch_shapes=[pltpu.VMEM((TOP_K, TM_C, d), jnp.float32), pltpu.SemaphoreType.DMA(())],
        compiler_params=cparams, name="moe_combine",
    )(dest, meta, x, ln_f.reshape(1, d), ys)


PAGES_PER_STEP = TK // PAGE_SIZE
ROWS_T = 8
ROWS_H = 16


def _page_specs(feat, n_pages):
    def spec(i):
        return pl.BlockSpec((1, feat, PAGE_SIZE),
                            lambda b, P, pt: (pt[b, jnp.minimum(P * PAGES_PER_STEP + i, n_pages - 1)], 0, 0))
    return [spec(i) for i in range(PAGES_PER_STEP)]


def _chunk(pages, new_ref, is_new):
    cache = jnp.concatenate([p[0] for p in pages], axis=1)
    return jnp.where(is_new, new_ref[0], cache)


def _idx_score_body(pt_ref, qi_ref, wi_ref, new_ref, *rest):
    del pt_ref
    pages, out_ref = rest[:PAGES_PER_STEP], rest[PAGES_PER_STEP]
    is_new = pl.program_id(1) == pl.num_programs(1) - 1
    kt = _chunk(pages, new_ref, is_new).astype(MXU_DTYPE)
    wi = wi_ref[0]
    sc = jnp.zeros((ROWS_T, TK), jnp.float32)
    for h in range(IDX_HEADS):
        sc = sc + jnp.maximum(jnp.dot(qi_ref[0, h], kt, preferred_element_type=jnp.float32), 0.0) * wi[:, h:h + 1]
    out_ref[0] = sc


def _topk_rows_body(sc_ref, out_ref, key_scr, *, nch, topk, past_len):
    t_io = lax.broadcasted_iota(jnp.int32, (TQ, TK), 0)
    j_io = lax.broadcasted_iota(jnp.int32, (TQ, TK), 1)
    qpos = past_len + t_io % ROWS_T

    def valid_chunk(c):
        return j_io + c * TK <= qpos

    def load(c, carry):
        key_scr[c] = jnp.where(valid_chunk(c), _order_key(sc_ref[c]), INT_MIN)
        return carry

    lax.fori_loop(0, nch, load, 0)
    thr2 = _select_topk(key_scr, 0, nch, topk, nch * TK)

    def finalize(c, carry):
        out_ref[c] = jnp.where(valid_chunk(c) & (key_scr[c] >= thr2), 1.0, 0.0)
        return carry

    lax.fori_loop(0, nch, finalize, 0)


def _decode_attn_body(pt_ref, q_ref, selblk_ref, e3_ref, dmask_ref, bias_ref, snew_ref, dnew_ref, *rest,
                      past_len):
    del pt_ref
    n = PAGES_PER_STEP
    spages, dpages, o_ref, m_scr, l_scr, acc_scr = rest[:n], rest[n:2 * n], rest[2 * n], *rest[2 * n + 1:]
    P = pl.program_id(1)
    last = pl.num_programs(1) - 1
    is_new = P == last

    @pl.when(P == 0)
    def _():
        m_scr[...] = jnp.full_like(m_scr, M_INIT)
        l_scr[...] = jnp.zeros_like(l_scr)
        acc_scr[...] = jnp.zeros_like(acc_scr)

    row = lax.broadcasted_iota(jnp.int32, (ROWS_H, TK), 0)
    kidx = lax.broadcasted_iota(jnp.int32, (ROWS_H, TK), 1) + P * TK
    causal = kidx <= past_len + row % (ROWS_H // NSA_REP)
    rep = jnp.where(lax.broadcasted_iota(jnp.int32, (ROWS_H, ROWS_T), 0) % (ROWS_H // DSA_REP)
                    == lax.broadcasted_iota(jnp.int32, (ROWS_H, ROWS_T), 1), 1.0, 0.0).astype(jnp.bfloat16)
    dm = jnp.dot(rep, dmask_ref[0].astype(jnp.bfloat16), preferred_element_type=jnp.float32)
    bsel = jnp.where(is_new, 2, jnp.where(P == last - 1, 1, 0))
    for br, (pages, new_ref) in enumerate(((spages, snew_ref), (dpages, dnew_ref))):
        data = _chunk(pages, new_ref, is_new)
        kt = data[:LANE].astype(MXU_DTYPE)
        vt = data[LANE:].astype(MXU_DTYPE)
        for g in range(NSA_KV):
            if br == 0:
                hit = jnp.dot(selblk_ref[0, g], e3_ref[0], preferred_element_type=jnp.float32)
                valid = (hit > 0.5) & causal
            else:
                valid = dm > 0.5
            s = (jnp.dot(q_ref[0, br, g], kt, preferred_element_type=jnp.float32) + bias_ref[br, g, bsel]
                 + jnp.where(valid, 0.0, NEG_INF))
            st = br * NSA_KV + g
            m_new = jnp.maximum(m_scr[st], jnp.max(s, axis=1, keepdims=True))
            alpha = jnp.exp(m_scr[st] - m_new)
            p = jnp.exp(s - m_new)
            l_scr[st] = alpha * l_scr[st] + jnp.sum(p, axis=1, keepdims=True)
            acc_scr[st] = alpha * acc_scr[st] + _dot_nt(p.astype(MXU_DTYPE), vt)
            m_scr[st] = m_new

    @pl.when(is_new)
    def _():
        for st in range(2 * NSA_KV):
            o_ref[0, st // NSA_KV, st % NSA_KV] = acc_scr[st] / l_scr[st]


def _token_minor(cache):
    n = cache.shape[0]
    return jnp.moveaxis(cache, 1, -1).reshape(n, -1, cache.shape[1])


def _new_chunk(rows):
    db, s = rows.shape[:2]
    t = jnp.moveaxis(rows.reshape(db, s, -1), 1, 2)
    return jnp.pad(t, ((0, 0), (0, 0), (0, TK - s)))


def _rows_rt(x, n_kv):
    db, s, hd = x.shape
    rep = hd // HEAD_DIM // n_kv
    xp = _pad_heads(x, n_kv).reshape(db, s, n_kv, rep, LANE)
    return jnp.transpose(xp, (0, 2, 3, 1, 4)).reshape(db, n_kv, rep * s, LANE)


def _sample_compress_body(pt_ref, wbd_ref, pe_ref, w2_ref, *rest):
    del pt_ref
    pages, o_ref, xs_scr, acc_scr = rest[:PAGES_PER_STEP], *rest[PAGES_PER_STEP:]
    P = pl.program_id(1)
    data = jnp.concatenate([p[0] for p in pages], axis=1)
    for f in range(xs_scr.shape[0]):
        xs_scr[f, pl.ds(pl.multiple_of(P * TK, TK), TK), :] = data[f * LANE:(f + 1) * LANE].T

    @pl.when(P == pl.num_programs(1) - 1)
    def _():
        n_chunks = xs_scr.shape[1] // CMP_STRIDE
        for s in range(CMP_STRIDE):
            rows = jnp.concatenate([xs_scr[f, pl.ds(s, n_chunks, stride=CMP_STRIDE), :]
                                    for f in range(xs_scr.shape[0])], axis=1)
            part = jnp.dot(rows.astype(MXU_DTYPE), wbd_ref[s], preferred_element_type=jnp.float32)
            if s == 0:
                acc_scr[...] = part
            else:
                acc_scr[...] += part
        half = acc_scr.shape[1] // 2
        hid = acc_scr[:, :half] + pltpu.roll(acc_scr[:, half:], n_chunks - 1, 0) + pe_ref[...]
        act = hid * jax.nn.sigmoid(hid)
        o_ref[0] = jnp.dot(act.astype(MXU_DTYPE), w2_ref[...], preferred_element_type=jnp.float32)


def sample_compress(cache_cmp, page_table, k_pe, k_w1, k_w2, v_pe, v_w1, v_w2):
    DB, n_pages = page_table.shape
    assert CMP_LEN == 2 * CMP_STRIDE and n_pages % PAGES_PER_STEP == 0
    past_len = n_pages * PAGE_SIZE
    n_chunks = past_len // CMP_STRIDE
    combos = 2 * NSA_KV
    feat = combos * HEAD_DIM
    w1 = jnp.stack([k_w1] * NSA_KV + [v_w1] * NSA_KV).reshape(combos, 2, CMP_STRIDE, HEAD_DIM, CMP_HIDDEN)
    wbd = jnp.einsum('cisdh,xc->sxdich', w1, jnp.eye(combos, dtype=w1.dtype))
    wbd = wbd.reshape(CMP_STRIDE, feat, 2 * combos * CMP_HIDDEN).astype(MXU_DTYPE)
    pe = jnp.concatenate([jnp.einsum('ld,ldh->h', k_pe, k_w1)] * NSA_KV + [jnp.einsum('ld,ldh->h', v_pe, v_w1)] * NSA_KV)
    w2 = jnp.stack([k_w2] * NSA_KV + [v_w2] * NSA_KV)
    w2bd = jnp.einsum('chd,xc->chxd', w2, jnp.eye(combos, dtype=w2.dtype)).reshape(combos * CMP_HIDDEN, feat)
    return pl.pallas_call(
        _sample_compress_body,
        grid_spec=pltpu.PrefetchScalarGridSpec(
            num_scalar_prefetch=1, grid=(DB, n_pages // PAGES_PER_STEP),
            in_specs=[pl.BlockSpec((CMP_STRIDE, feat, 2 * combos * CMP_HIDDEN), lambda b, P, pt: (0, 0, 0)),
                      pl.BlockSpec((1, combos * CMP_HIDDEN), lambda b, P, pt: (0, 0)),
                      pl.BlockSpec((combos * CMP_HIDDEN, feat), lambda b, P, pt: (0, 0))]
            + _page_specs(feat, n_pages),
            out_specs=pl.BlockSpec((1, n_chunks, feat), lambda b, P, pt: (b, 0, 0)),
            scratch_shapes=[pltpu.VMEM((feat // LANE, past_len, LANE), jnp.float32),
                            pltpu.VMEM((n_chunks, 2 * combos * CMP_HIDDEN), jnp.float32)]),
        out_shape=jax.ShapeDtypeStruct((DB, n_chunks, feat), jnp.float32),
        compiler_params=pltpu.CompilerParams(dimension_semantics=("arbitrary", "arbitrary"),
                                             vmem_limit_bytes=VMEM_LIMIT),
        name="sample_compress",
    )(page_table, wbd, pe.reshape(1, -1), w2bd.astype(MXU_DTYPE), *([_token_minor(cache_cmp)] * PAGES_PER_STEP))


def sample_sparse_attention(q_n, q_d, qi, wi, ki_new, selblk, sel_new, dsa_new, cache_sel, cache_dsa, cache_idx,
                            page_table, rel_bias):
    DB, S, _ = q_n.shape
    n_pages = page_table.shape[1]
    past_len = n_pages * PAGE_SIZE
    assert n_pages % PAGES_PER_STEP == 0 and S <= ROWS_T and NSA_REP * S == ROWS_H and DSA_REP * S == ROWS_H
    nch = n_pages // PAGES_PER_STEP + 1
    topk = min(DSA_TOPK_MAX, (past_len + S) // 4)
    cparams = pltpu.CompilerParams(dimension_semantics=("arbitrary", "arbitrary"), vmem_limit_bytes=VMEM_LIMIT)

    qi8 = jnp.pad(jnp.transpose(qi.reshape(DB, S, IDX_HEADS, IDX_DIM), (0, 2, 1, 3)) * IDX_DIM ** -0.5,
                  ((0, 0), (0, 0), (0, ROWS_T - S), (0, 0))).astype(MXU_DTYPE)
    wi8 = jnp.pad(wi, ((0, 0), (0, ROWS_T - S), (0, 0)))
    scores = pl.pallas_call(
        _idx_score_body,
        grid_spec=pltpu.PrefetchScalarGridSpec(
            num_scalar_prefetch=1, grid=(DB, nch),
            in_specs=[pl.BlockSpec((1, IDX_HEADS, ROWS_T, IDX_DIM), lambda b, P, pt: (b, 0, 0, 0)),
                      pl.BlockSpec((1, ROWS_T, IDX_HEADS), lambda b, P, pt: (b, 0, 0)),
                      pl.BlockSpec((1, IDX_DIM, TK), lambda b, P, pt: (b, 0, 0))] + _page_specs(IDX_DIM, n_pages),
            out_specs=pl.BlockSpec((1, ROWS_T, TK), lambda b, P, pt: (P, b, 0))),
        out_shape=jax.ShapeDtypeStruct((nch, DB * ROWS_T, TK), jnp.float32),
        compiler_params=cparams, name="sample_idx_scores",
    )(page_table, qi8, wi8, _new_chunk(ki_new), *([_token_minor(cache_idx)] * PAGES_PER_STEP))

    rows = DB * ROWS_T
    assert rows % TQ == 0
    dmask = pl.pallas_call(
        partial(_topk_rows_body, nch=nch, topk=topk, past_len=past_len),
        grid=(rows // TQ,),
        in_specs=[pl.BlockSpec((nch, TQ, TK), lambda i: (0, i, 0))],
        out_specs=pl.BlockSpec((nch, TQ, TK), lambda i: (0, i, 0)),
        out_shape=jax.ShapeDtypeStruct((nch, rows, TK), jnp.float32),
        scratch_shapes=[pltpu.VMEM((nch, TQ, TK), jnp.int32)],
        compiler_params=pltpu.CompilerParams(dimension_semantics=("arbitrary",), vmem_limit_bytes=VMEM_LIMIT),
        name="sample_topk_mask",
    )(scores)

    q2 = jnp.stack([_rows_rt(q_n * HEAD_DIM ** -0.5, NSA_KV), _rows_rt(q_d * HEAD_DIM ** -0.5, DSA_KV)],
                   axis=1).astype(MXU_DTYPE)
    nsb = selblk.shape[-1]
    nsb_pad = -(-(nch * TK // SEL_BLOCK) // LANE) * LANE
    sb = jnp.pad(jnp.transpose(selblk, (0, 2, 1, 3)), ((0, 0), (0, 0), (0, 0), (0, nsb_pad - nsb)))
    sb = jnp.tile(sb, (1, 1, NSA_REP, 1)).astype(jnp.bfloat16)
    e3 = (np.arange(nch * TK)[None, :] // SEL_BLOCK == np.arange(nsb_pad)[:, None])
    e3 = jnp.asarray(e3.reshape(nsb_pad, nch, TK).transpose(1, 0, 2), jnp.bfloat16)
    t = np.tile(np.arange(S), ROWS_H // S)[:, None]
    j = np.arange(TK)[None, :]
    dist = np.stack([np.full((ROWS_H, TK), REL_MAX_DIST), t + TK - j, t - j])
    bias = _bias_table(rel_bias, dist)
    r_of_row = np.arange(ROWS_H) // S
    bias = jnp.stack([jnp.stack([jnp.stack([bias[br * NSA_HEADS + g * NSA_REP + r_of_row, v, np.arange(ROWS_H)]
                                            for v in range(3)]) for g in range(NSA_KV)]) for br in range(2)])
    feat = 2 * NSA_KV * HEAD_DIM
    out = pl.pallas_call(
        partial(_decode_attn_body, past_len=past_len),
        grid_spec=pltpu.PrefetchScalarGridSpec(
            num_scalar_prefetch=1, grid=(DB, nch),
            in_specs=[pl.BlockSpec((1, 2, NSA_KV, ROWS_H, LANE), lambda b, P, pt: (b, 0, 0, 0, 0)),
                      pl.BlockSpec((1, NSA_KV, ROWS_H, nsb_pad), lambda b, P, pt: (b, 0, 0, 0)),
                      pl.BlockSpec((1, nsb_pad, TK), lambda b, P, pt: (P, 0, 0)),
                      pl.BlockSpec((1, ROWS_T, TK), lambda b, P, pt: (P, b, 0)),
                      pl.BlockSpec((2, NSA_KV, 3, ROWS_H, TK), lambda b, P, pt: (0, 0, 0, 0, 0)),
                      pl.BlockSpec((1, feat, TK), lambda b, P, pt: (b, 0, 0)),
                      pl.BlockSpec((1, feat, TK), lambda b, P, pt: (b, 0, 0))]
            + _page_specs(feat, n_pages) + _page_specs(feat, n_pages),
            out_specs=pl.BlockSpec((1, 2, NSA_KV, ROWS_H, LANE), lambda b, P, pt: (b, 0, 0, 0, 0)),
            scratch_shapes=[pltpu.VMEM((2 * NSA_KV, ROWS_H, 1), jnp.float32),
                            pltpu.VMEM((2 * NSA_KV, ROWS_H, 1), jnp.float32),
                            pltpu.VMEM((2 * NSA_KV, ROWS_H, LANE), jnp.float32)]),
        out_shape=jax.ShapeDtypeStruct((DB, 2, NSA_KV, ROWS_H, LANE), jnp.float32),
        compiler_params=cparams, name="sample_decode_attn",
    )(page_table, q2, sb, e3, dmask, bias, _new_chunk(sel_new), _new_chunk(dsa_new),
      *([_token_minor(cache_sel)] * PAGES_PER_STEP), *([_token_minor(cache_dsa)] * PAGES_PER_STEP))
    o = out.reshape(DB, 2, NSA_KV, NSA_REP, S, NSA_KV, HEAD_DIM)
    o = jnp.stack([o[:, :, g, :, :, g, :] for g in range(NSA_KV)], axis=2)
    o = jnp.transpose(o, (1, 0, 4, 2, 3, 5)).reshape(2, DB, S, NSA_HEADS * HEAD_DIM)
    return o[0], o[1]


def rmsnorm(x, g):
    xf = x.astype(jnp.float32)
    y = xf * lax.rsqrt(jnp.mean(xf * xf, axis=-1, keepdims=True) + RMS_EPS)
    return (y * g.astype(jnp.float32)).astype(x.dtype)


def split_proj(h):
    out, off = {}, 0
    for name, n in SPLITS:
        out[name] = h[..., off:off + n]
        off += n
    return out


def t5_bucket(dist):
    n = jnp.maximum(dist, 0)
    max_exact = N_BUCKETS // 2
    nf = jnp.maximum(n, 1).astype(jnp.float32)
    large = max_exact + (jnp.log(nf / max_exact) / math.log(REL_MAX_DIST / max_exact)
                         * (N_BUCKETS - max_exact)).astype(jnp.int32)
    return jnp.where(n < max_exact, n, jnp.minimum(large, N_BUCKETS - 1))


def masked_softmax(s, mask):
    p = jax.nn.softmax(jnp.where(mask, s, NEG_INF), axis=-1)
    return jnp.where(mask, p, 0.0)


def cmp_to_sel_matrix(nc, ns):
    c0 = np.arange(nc) * CMP_STRIDE
    s0 = np.arange(ns) * SEL_BLOCK
    ov = np.minimum(c0[:, None] + CMP_LEN, s0[None, :] + SEL_BLOCK) - np.maximum(c0[:, None], s0[None, :])
    return jnp.asarray(np.clip(ov, 0, None) / CMP_LEN, dtype=jnp.float32)


def compress(rows, pe, w1, w2):
    B, L, G, D = rows.shape
    r = CMP_LEN // CMP_STRIDE
    nch = L // CMP_STRIDE
    nc = nch - r + 1
    ch = rows[:, :nch * CMP_STRIDE].reshape(B, nch, CMP_STRIDE, G, D)
    w1c = w1.reshape(r, CMP_STRIDE, D, CMP_HIDDEN)
    hid = jnp.einsum('ld,ldh->h', pe, w1)
    for i in range(r):
        hid = hid + jnp.einsum('bnsgd,sdh->bngh', ch[:, i:i + nc], w1c[i])
    return jax.nn.silu(hid) @ w2


def nsa_dense_branches(q, q_pos, gates, kc, vc, c_end, n_keys, win_rows, w_pos, rel_tab):
    B, T = q.shape[:2]
    scale = HEAD_DIM ** -0.5
    qg = q.reshape(B, T, NSA_KV, NSA_REP, HEAD_DIM)
    d_c = q_pos[:, None] - c_end[None, :]
    s = jnp.einsum('btgrd,bcgd->btgrc', qg, kc).astype(jnp.float32) * scale
    s = s + jnp.moveaxis(rel_tab[t5_bucket(d_c)], 1, -1)
    p_c = masked_softmax(s, (d_c >= 0)[:, None, None, :])
    o_c = jnp.einsum('btgrc,bcgd->btgrd', p_c.astype(vc.dtype), vc)
    nc = kc.shape[1]
    ns = -(-n_keys // SEL_BLOCK)
    blk = jnp.einsum('btgrc,cj->btgj', p_c, cmp_to_sel_matrix(nc, ns))
    j = jnp.arange(ns)
    cur = (q_pos // SEL_BLOCK)[:, None]
    visible = ((j * SEL_BLOCK)[None, :] <= q_pos[:, None])[:, None, :]
    forced = ((j[None, :] == 0) | (j[None, :] == cur) | (j[None, :] == cur - 1))[:, None, :]
    score = jnp.where(visible, jnp.where(forced, FORCED_SCORE, blk), -jnp.inf)
    _, idx = lax.top_k(score, min(N_SEL, ns))
    d_w = q_pos[:, None] - w_pos[None, :]
    s = jnp.einsum('btgrd,bwgd->btgrw', qg, win_rows[:, :, 0]).astype(jnp.float32) * scale
    s = s + jnp.moveaxis(rel_tab[t5_bucket(d_w)], 1, -1)
    ok_w = (d_w >= 0) & (d_w <= WINDOW) & (w_pos >= 0)[None, :]
    p_w = masked_softmax(s, ok_w[:, None, None, :])
    o_w = jnp.einsum('btgrw,bwgd->btgrd', p_w.astype(win_rows.dtype), win_rows[:, :, 1])
    g = gates.reshape(B, T, NSA_KV, NSA_REP, 3).astype(o_c.dtype)
    return o_c, o_w, g, idx


def mem_attend(h, mem_kv, w_q, w_o):
    B, T = h.shape[:2]
    q = (h @ w_q).reshape(B, T, X_HEADS, X_HEAD_DIM)
    s = jnp.einsum('bthd,bmhd->bthm', q, mem_kv[:, :, 0]).astype(jnp.float32) * X_HEAD_DIM ** -0.5
    p = jax.nn.softmax(s, axis=-1)
    o = jnp.einsum('bthm,bmhd->bthd', p.astype(mem_kv.dtype), mem_kv[:, :, 1])
    return o.reshape(B, T, X_HEADS * X_HEAD_DIM) @ w_o


def layer_tail(x, o_n, o_d, merge_gate, mem_kv, w_nsa_out, w_dsa_out, w_o, ln_mem, w_mem_q, w_mem_o):
    g = jax.nn.sigmoid(merge_gate)
    x = x + (g[..., :D_MODEL] * (o_n @ w_nsa_out) + g[..., D_MODEL:] * (o_d @ w_dsa_out)) @ w_o
    return x + mem_attend(rmsnorm(x, ln_mem), mem_kv, w_mem_q, w_mem_o)


def kernel(x_prompt, x_sample, cache_nsa_cmp_kv, cache_nsa_sel_kv, cache_nsa_win_kv, cache_dsa_kv,
           cache_dsa_idx_k, cache_mem_kv, page_table, mem_prompt, ln_mix, w_in, cmp_k_pe, cmp_k_w1, cmp_k_w2,
           cmp_v_pe, cmp_v_w1, cmp_v_w2, rel_bias, w_nsa_out, w_dsa_out, w_o, ln_mem, w_mem_q, w_mem_kv,
           w_mem_o, ln_ffn, w_router, b_router, w_gu, b_gu, w_down, b_down, ln_f):
    rel_nsa = rel_bias[:, :NSA_HEADS].reshape(N_BUCKETS, NSA_KV, NSA_REP)
    rel_dsa = rel_bias[:, NSA_HEADS:].reshape(N_BUCKETS, DSA_KV, DSA_REP)
    tail_w = (w_nsa_out, w_dsa_out, w_o, ln_mem, w_mem_q, w_mem_o)

    B, L = x_prompt.shape[:2]
    pp = split_proj(norm_proj(x_prompt.reshape(B * L, D_MODEL), ln_mix, w_in).reshape(B, L, D_IN))
    kv6 = pp["nsa_kv"].reshape(B, L, 6, NSA_KV, HEAD_DIM)
    kc = compress(kv6[:, :, 0], cmp_k_pe, cmp_k_w1, cmp_k_w2)
    vc = compress(kv6[:, :, 1], cmp_v_pe, cmp_v_w1, cmp_v_w2)
    k6 = pp["nsa_kv"].reshape(B, L, 6, NSA_KV * HEAD_DIM)
    NC = kc.shape[1]
    o_n = nsa_prompt(pp["nsa_q"], pp["nsa_gate"], kc.reshape(B, NC, NSA_KV * HEAD_DIM),
                     vc.reshape(B, NC, NSA_KV * HEAD_DIM), k6[:, :, 2], k6[:, :, 3], k6[:, :, 4], k6[:, :, 5],
                     rel_bias[:, :NSA_HEADS])
    dsa_rows = pp["dsa_kv"].reshape(B, L, 2, DSA_KV, HEAD_DIM)
    dk = pp["dsa_kv"].reshape(B, L, 2, DSA_KV * HEAD_DIM)
    ki = pp["idx_k"]
    o_d = dsa_prompt(pp["dsa_q"], pp["idx_q"], pp["idx_w"], ki, dk[:, :, 0], dk[:, :, 1], rel_bias[:, NSA_HEADS:])
    mem_kv_p = (mem_prompt @ w_mem_kv).reshape(B, MEM_LEN, 2, X_HEADS, X_HEAD_DIM)
    x2_prompt = layer_tail(x_prompt, o_n, o_d, pp["merge_gate"], mem_kv_p, *tail_w)

    DB, S = x_sample.shape[:2]
    past_len = page_table.shape[1] * cache_dsa_kv.shape[1]
    n_keys = past_len + S
    q_pos = past_len + jnp.arange(S)
    ps = split_proj(norm_proj(x_sample.reshape(DB * S, D_MODEL), ln_mix, w_in).reshape(DB, S, D_IN))
    q_ns = ps["nsa_q"].reshape(DB, S, NSA_HEADS, HEAD_DIM)
    g_ns = jax.nn.sigmoid(ps["nsa_gate"]).reshape(DB, S, NSA_HEADS, 3)
    kv6s = ps["nsa_kv"].reshape(DB, S, 6, NSA_KV, HEAD_DIM)
    assert (past_len + S) // CMP_STRIDE == past_len // CMP_STRIDE
    n_cmp = past_len // CMP_STRIDE - CMP_LEN // CMP_STRIDE + 1
    kvc = sample_compress(cache_nsa_cmp_kv, page_table, cmp_k_pe, cmp_k_w1, cmp_k_w2, cmp_v_pe, cmp_v_w1, cmp_v_w2)
    kvc = kvc[:, :n_cmp].reshape(DB, n_cmp, 2, NSA_KV, HEAD_DIM)
    kc_s, vc_s = kvc[:, :, 0], kvc[:, :, 1]
    c_end_s = jnp.arange(kc_s.shape[1]) * CMP_STRIDE + CMP_LEN - 1
    win_rows_s = jnp.concatenate([cache_nsa_win_kv, kv6s[:, :, 4:6]], axis=1)
    wb = cache_nsa_win_kv.shape[1]
    w_pos_s = past_len - wb + jnp.arange(wb + S)
    o_cs, o_ws, g_s, sel_idx = nsa_dense_branches(q_ns, q_pos, g_ns, kc_s, vc_s, c_end_s, n_keys, win_rows_s,
                                                  w_pos_s, rel_nsa)
    n_blk = -(-n_keys // SEL_BLOCK)
    selblk = jnp.sum(jax.nn.one_hot(sel_idx, n_blk, dtype=jnp.float32), axis=-2)
    dsa_rows_s = ps["dsa_kv"].reshape(DB, S, 2, DSA_KV, HEAD_DIM)
    o_ss, o_ds = sample_sparse_attention(ps["nsa_q"], ps["dsa_q"], ps["idx_q"], ps["idx_w"], ps["idx_k"], selblk,
                                         kv6s[:, :, 2:4], dsa_rows_s, cache_nsa_sel_kv, cache_dsa_kv,
                                         cache_dsa_idx_k, page_table, rel_bias)
    o_ns = (o_cs * g_s[..., 0:1] + o_ss.reshape(o_cs.shape) * g_s[..., 1:2] + o_ws * g_s[..., 2:3])
    o_ns = o_ns.reshape(DB, S, NSA_HEADS * HEAD_DIM)
    x2_sample = layer_tail(x_sample, o_ns, o_ds, ps["merge_gate"], cache_mem_kv, *tail_w)

    x2 = jnp.concatenate([x2_prompt.reshape(B * L, D_MODEL), x2_sample.reshape(DB * S, D_MODEL)], axis=0)
    y = moe_block(x2, ln_ffn, w_router, b_router, w_gu, b_gu, w_down, b_down, ln_f)
    y_prompt = y[:B * L].reshape(B, L, D_MODEL)
    y_sample = y[B * L:].reshape(DB, S, D_MODEL)
    L_w = min(WINDOW, L)
    return (y_prompt, y_sample,
            kv6[:, :, 0:2], kv6[:, :, 2:4], kv6[:, L - L_w:, 4:6], dsa_rows, ki, mem_kv_p,
            kv6s[:, :, 0:2], kv6s[:, :, 2:4], win_rows_s[:, -wb:], dsa_rows_s, ps["idx_k"])
```

```python
import math
from functools import partial

import jax
import jax.numpy as jnp
import numpy as np
from jax import lax
from jax.experimental import pallas as pl
from jax.experimental.pallas import tpu as pltpu

D_MODEL = 1024
PAGE_SIZE = 128
HEAD_DIM = 64
NSA_HEADS = 8
NSA_KV = 2
NSA_REP = NSA_HEADS // NSA_KV
CMP_LEN = 32
CMP_STRIDE = 16
CMP_HIDDEN = 128
SEL_BLOCK = 64
N_SEL = 16
WINDOW = 512
FORCED_SCORE = 1e4
DSA_HEADS = 8
DSA_KV = 2
DSA_REP = DSA_HEADS // DSA_KV
IDX_HEADS = 4
IDX_DIM = 64
DSA_TOPK_MAX = 256
MEM_LEN = 256
X_HEADS = 4
X_HEAD_DIM = 64
N_EXPERTS = 32
TOP_K = 4
D_FF = 1024
SWIGLU_LIMIT = 7.0
SWIGLU_ALPHA = 1.702
MOE_BLOCK = 128
N_BUCKETS = 32
REL_MAX_DIST = 128
Q_BLOCK = 128
RMS_EPS = 1e-6
NEG_INF = -1e30
N_REL_HEADS = NSA_HEADS + DSA_HEADS
SPLITS = (("nsa_q", NSA_HEADS * HEAD_DIM), ("nsa_kv", 6 * NSA_KV * HEAD_DIM), ("nsa_gate", 3 * NSA_HEADS),
          ("dsa_q", DSA_HEADS * HEAD_DIM), ("dsa_kv", 2 * DSA_KV * HEAD_DIM), ("idx_q", IDX_HEADS * IDX_DIM),
          ("idx_k", IDX_DIM), ("idx_w", IDX_HEADS), ("merge_gate", 2 * D_MODEL))
D_IN = sum(n for _, n in SPLITS)

LANE = 128
VMEM_LIMIT = 48 * 1024 * 1024


def _norm_proj_body(x_ref, g_ref, w_ref, o_ref):
    x = x_ref[...]
    y = x * lax.rsqrt(jnp.mean(x * x, axis=-1, keepdims=True) + RMS_EPS) * g_ref[...]
    o_ref[...] = jnp.dot(y.astype(w_ref.dtype), w_ref[...], preferred_element_type=jnp.float32)


def norm_proj(x, g, w, tm=512, tn=640):
    n, d = x.shape
    m = w.shape[1]
    m_pad = -(-m // tn) * tn
    wb = jnp.pad(w, ((0, 0), (0, m_pad - m))).astype(MXU_DTYPE)
    tm = min(tm, n)
    out = pl.pallas_call(
        _norm_proj_body,
        grid=(n // tm, m_pad // tn),
        in_specs=[pl.BlockSpec((tm, d), lambda i, j: (i, 0)),
                  pl.BlockSpec((1, d), lambda i, j: (0, 0)),
                  pl.BlockSpec((d, tn), lambda i, j: (0, j))],
        out_specs=pl.BlockSpec((tm, tn), lambda i, j: (i, j)),
        out_shape=jax.ShapeDtypeStruct((n, m_pad), jnp.float32),
        compiler_params=pltpu.CompilerParams(dimension_semantics=("arbitrary", "arbitrary"),
                                             vmem_limit_bytes=VMEM_LIMIT),
        name="norm_proj",
    )(x, g.reshape(1, d), wb)
    return out[:, :m]


MXU_DTYPE = jnp.bfloat16
TQ = 128
TK = 512
INT_MIN = -2 ** 31
M_INIT = -5e29
MAX_EXACT = N_BUCKETS // 2


def _dot_nt(a, b):
    return lax.dot_general(a, b, (((1,), (1,)), ((), ())), preferred_element_type=jnp.float32)


def _bucket_np(d):
    n = np.maximum(d, 0)
    nf = np.maximum(n, 1).astype(np.float64)
    large = MAX_EXACT + (np.log(nf / MAX_EXACT) / math.log(REL_MAX_DIST / MAX_EXACT)
                         * (N_BUCKETS - MAX_EXACT)).astype(np.int64)
    return np.where(n < MAX_EXACT, n, np.minimum(large, N_BUCKETS - 1))


def _bias_table(rel, dist):
    return jnp.take(rel.T, jnp.asarray(_bucket_np(dist), jnp.int32), axis=1)


def _flash_step(q, k, v, bias, madd, m, l, acc):
    nh = q.shape[0] // madd.shape[0]
    s = _dot_nt(q, k)
    s = (s.reshape(nh, *madd.shape) + madd[None]).reshape(s.shape) + bias
    m_new = jnp.maximum(m, jnp.max(s, axis=1, keepdims=True))
    alpha = jnp.exp(m - m_new)
    p = jnp.exp(s - m_new)
    l = alpha * l + jnp.sum(p, axis=1, keepdims=True)
    acc = alpha * acc + jnp.dot(p.astype(v.dtype), v, preferred_element_type=jnp.float32)
    return m_new, l, acc


def _masked_branch(q, k_ref, v_ref, madd_scr, bias_last, bias_far, start, c_lo, nch):
    def body(c, carry):
        r0 = pl.multiple_of(start + c * TK, TQ)
        return _flash_step(q, k_ref[0, pl.ds(r0, TK), :], v_ref[0, pl.ds(r0, TK), :], bias_far, madd_scr[c], *carry)

    rows = q.shape[0]
    init = (jnp.full((rows, 1), M_INIT, jnp.float32), jnp.zeros((rows, 1), jnp.float32),
            jnp.zeros((rows, LANE), jnp.float32))
    carry = lax.fori_loop(c_lo, nch - 1, body, init)
    r0 = pl.multiple_of(start + (nch - 1) * TK, TQ)
    m, l, acc = _flash_step(q, k_ref[0, pl.ds(r0, TK), :], v_ref[0, pl.ds(r0, TK), :], bias_last,
                            madd_scr[nch - 1], *carry)
    return acc / l


def _group_rows(ref, g, rep, cols=None):
    blk = ref[g * rep:(g + 1) * rep] if cols is None else ref[g * rep:(g + 1) * rep, :, cols]
    return blk.reshape(rep * TQ, blk.shape[-1])


def _group_q(q_ref, g, rep):
    return jnp.concatenate([q_ref[0, :, LANE * h:LANE * (h + 1)] for h in range(g * rep, (g + 1) * rep)], axis=0)


def _select_topk(key_scr, c_lo, nch, topk, n_keys):
    j_io = lax.broadcasted_iota(jnp.int32, (TQ, TK), 1)

    def count_ge(cand):
        def body(c, acc):
            x = jnp.where(key_scr[c] >= cand, 1.0, 0.0)
            for j in range(TK // LANE):
                acc = acc + x[:, LANE * j:LANE * (j + 1)]
            return acc
        acc = lax.fori_loop(c_lo, nch, body, jnp.zeros((TQ, LANE), jnp.float32))
        return jnp.sum(acc, axis=1, keepdims=True)

    def search(nbits, init):
        def body(i, cur):
            cand = cur + lax.shift_left(jnp.int32(1), nbits - 1 - i)
            return jnp.where(count_ge(cand) >= float(topk), cand, cur)
        return lax.fori_loop(0, nbits, body, init)

    thr = search(32, jnp.full((TQ, 1), INT_MIN, jnp.int32))
    any_ties = jnp.max(count_ge(thr)) > float(topk)
    nb2 = max((n_keys - 1).bit_length(), 1)

    def rekey(c, carry):
        kv = key_scr[c]
        kw = j_io + c * TK
        key_scr[c] = jnp.where(kv > thr, 2 ** nb2, jnp.where(kv == thr, (n_keys - 1) - kw, -1))
        return carry

    lax.fori_loop(c_lo, nch, rekey, 0)
    zero = jnp.zeros((TQ, 1), jnp.int32)
    return lax.cond(any_ties, lambda: search(nb2, zero), lambda: zero)


def _order_key(sc):
    bits = lax.bitcast_convert_type(jnp.where(sc == 0.0, 0.0, sc), jnp.int32)
    return bits ^ ((bits >> 31) & 0x7FFFFFFF)


def _dsa_body(qi_ref, wi_ref, ki_ref, q_ref, k_ref, v_ref, bias_ref, o_ref, key_scr, madd_scr, *, L, topk):
    qb = pl.program_id(1)
    start = qb * TQ
    nch = L // TK
    off = L - TQ
    c_lo = (off - start) // TK
    t_io = lax.broadcasted_iota(jnp.int32, (TQ, TK), 0)
    j_io = lax.broadcasted_iota(jnp.int32, (TQ, TK), 1)
    lane_hi = lax.broadcasted_iota(jnp.int32, (TQ, LANE), 1) >= IDX_DIM
    wi = wi_ref[0]
    qis = []
    for pr in range(IDX_HEADS // 2):
        qp = qi_ref[0, :, LANE * pr:LANE * (pr + 1)]
        qis.append(jnp.where(lane_hi, jnp.zeros_like(qp), qp))
        qis.append(jnp.where(lane_hi, qp, jnp.zeros_like(qp)))

    def valid_chunk(c):
        kpos = j_io + (c * TK + start - off)
        return (kpos >= 0) & (kpos <= t_io + start)

    def score_chunk(c, carry):
        r0 = pl.multiple_of(start + c * TK, TQ)
        kk = ki_ref[0, pl.ds(r0, TK), :]
        sc = jnp.zeros((TQ, TK), jnp.float32)
        for h in range(IDX_HEADS):
            sc = sc + jnp.maximum(_dot_nt(qis[h], kk), 0.0) * wi[:, h:h + 1]
        key_scr[c] = jnp.where(valid_chunk(c), _order_key(sc), INT_MIN)
        return carry

    lax.fori_loop(c_lo, nch, score_chunk, 0)

    thr2 = _select_topk(key_scr, c_lo, nch, topk, L)

    def finalize(c, carry):
        madd_scr[c] = jnp.where(valid_chunk(c) & (key_scr[c] >= thr2), 0.0, NEG_INF)
        return carry

    lax.fori_loop(c_lo, nch, finalize, 0)

    for g in range(DSA_KV):
        o = _masked_branch(_group_q(q_ref, g, DSA_REP), k_ref, v_ref, madd_scr,
                           _group_rows(bias_ref, g, DSA_REP, slice(0, TK)),
                           _group_rows(bias_ref, g, DSA_REP, slice(TK, TK + 1)), start, c_lo, nch)
        for r in range(DSA_REP):
            o_ref[0, g * DSA_REP + r] = o[r * TQ:(r + 1) * TQ].astype(o_ref.dtype)


def _nsa_body(q_ref, gate_ref, kc_ref, vc_ref, ks_ref, vs_ref, kw_ref, vw_ref, cb_ref, mfix_ref, e3_ref,
              sbias_ref, wbias_ref, o_ref, madd_scr, *, L, WC, NS, n_sel):
    qb = pl.program_id(1)
    start = qb * TQ
    nch = L // TK
    off = L - TQ
    c_lo = (off - start) // TK
    WW = WINDOW + TQ
    t_io = lax.broadcasted_iota(jnp.int32, (TQ, TK), 0)
    j_io = lax.broadcasted_iota(jnp.int32, (TQ, TK), 1)
    gates = jax.nn.sigmoid(gate_ref[0])

    r8 = pl.multiple_of(qb * (TQ // CMP_STRIDE), 8)
    kcw = kc_ref[0, pl.ds(r8, WC), :].astype(MXU_DTYPE)
    vcw = vc_ref[0, pl.ds(r8, WC), :].astype(MXU_DTYPE)
    t_c = lax.broadcasted_iota(jnp.int32, (TQ, WC), 0)
    u_c = lax.broadcasted_iota(jnp.int32, (TQ, WC), 1)
    c_idx = u_c + (qb * (TQ // CMP_STRIDE) - (WC - TQ // CMP_STRIDE))
    d_c = t_c - CMP_STRIDE * (u_c - (WC - TQ // CMP_STRIDE)) - (CMP_LEN - 1)
    valid_c = (c_idx >= 0) & (d_c >= 0)

    t_w = lax.broadcasted_iota(jnp.int32, (TQ, WW), 0)
    j_w = lax.broadcasted_iota(jnp.int32, (TQ, WW), 1)
    d_w = t_w + WINDOW - j_w
    valid_w = (d_w >= 0) & (d_w <= WINDOW) & (j_w + (start - WINDOW) >= 0)
    madd_w = jnp.where(valid_w, 0.0, NEG_INF)
    rw = pl.multiple_of(start, TQ)
    kwin = kw_ref[0, pl.ds(rw, WW), :]
    vwin = vw_ref[0, pl.ds(rw, WW), :]

    jj = lax.broadcasted_iota(jnp.int32, (TQ, NS), 1)
    qpos = lax.broadcasted_iota(jnp.int32, (TQ, NS), 0) + start
    j_abs = jj + (qb * (TQ // SEL_BLOCK) - (NS - TQ // SEL_BLOCK))
    cur = qpos // SEL_BLOCK
    visible = (j_abs >= 0) & (j_abs * SEL_BLOCK <= qpos)
    forced = (j_abs == 0) | (j_abs == cur) | (j_abs == cur - 1)
    mfix = mfix_ref[...]

    for g in range(NSA_KV):
        psum = jnp.zeros((TQ, WC), jnp.float32)
        o_cs = []
        for r in range(NSA_REP):
            h = g * NSA_REP + r
            q = q_ref[0, :, LANE * h:LANE * (h + 1)]
            s = jnp.where(valid_c, _dot_nt(q, kcw) + cb_ref[h], NEG_INF)
            m = jnp.max(s, axis=1, keepdims=True)
            p = jnp.where(valid_c, jnp.exp(s - m), 0.0)
            l = jnp.sum(p, axis=1, keepdims=True)
            pn = p * jnp.where(l > 0.0, 1.0 / l, 0.0)
            psum = psum + pn
            o_cs.append(jnp.dot(pn.astype(MXU_DTYPE), vcw, preferred_element_type=jnp.float32))
        h1 = psum.astype(jnp.bfloat16)
        r1 = psum - h1.astype(jnp.float32)
        h2 = r1.astype(jnp.bfloat16)
        h3 = (r1 - h2.astype(jnp.float32)).astype(jnp.bfloat16)
        blk = (jnp.dot(h1, mfix, preferred_element_type=jnp.float32)
               + jnp.dot(h2, mfix, preferred_element_type=jnp.float32)
               + jnp.dot(h3, mfix, preferred_element_type=jnp.float32))
        score = jnp.where(visible, jnp.where(forced, FORCED_SCORE, blk), -jnp.inf)
        rank = jnp.zeros((TQ, NS), jnp.float32)
        for i in range(NS):
            col = score[:, i:i + 1]
            rank = rank + jnp.where((col > score) | ((col == score) & (jj > i)), 1.0, 0.0)
        sel = jnp.where(rank < float(n_sel), 1.0, 0.0).astype(jnp.bfloat16)

        def expand(c, carry):
            me = jnp.dot(sel, e3_ref[c], preferred_element_type=jnp.float32)
            kpos = j_io + (c * TK + start - off)
            ok = (me > 0.5) & (kpos >= 0) & (kpos <= t_io + start)
            madd_scr[c] = jnp.where(ok, 0.0, NEG_INF)
            return carry

        lax.fori_loop(c_lo, nch, expand, 0)

        q4 = _group_q(q_ref, g, NSA_REP)
        o_s = _masked_branch(q4, ks_ref, vs_ref, madd_scr, _group_rows(sbias_ref, g, NSA_REP, slice(0, TK)),
                             _group_rows(sbias_ref, g, NSA_REP, slice(TK, TK + 1)), start, c_lo, nch)
        s = _dot_nt(q4, kwin)
        s = (s.reshape(NSA_REP, TQ, WW) + madd_w[None]).reshape(s.shape) + _group_rows(wbias_ref, g, NSA_REP)
        p = jnp.exp(s - jnp.max(s, axis=1, keepdims=True))
        o_w = (jnp.dot(p.astype(vwin.dtype), vwin, preferred_element_type=jnp.float32)
               / jnp.sum(p, axis=1, keepdims=True))
        for r in range(NSA_REP):
            h = g * NSA_REP + r
            rows = slice(r * TQ, (r + 1) * TQ)
            o = (o_cs[r] * gates[:, 3 * h:3 * h + 1] + o_s[rows] * gates[:, 3 * h + 1:3 * h + 2]
                 + o_w[rows] * gates[:, 3 * h + 2:3 * h + 3])
            o_ref[0, h] = o.astype(o_ref.dtype)


def _pad_heads(q, n_kv):
    B, L, hd = q.shape
    rep = hd // HEAD_DIM // n_kv
    eye = jnp.eye(n_kv, dtype=q.dtype).reshape(1, 1, n_kv, 1, n_kv, 1)
    return (q.reshape(B, L, n_kv, rep, 1, HEAD_DIM) * eye).reshape(B, L, n_kv * rep * n_kv * HEAD_DIM)


def _unpad_heads(o, n_kv):
    B, H, L, _ = o.shape
    rep = H // n_kv
    o6 = o.reshape(B, n_kv, rep, L, n_kv, HEAD_DIM)
    oo = jnp.stack([o6[:, g, :, :, g, :] for g in range(n_kv)], axis=1)
    return jnp.transpose(oo, (0, 3, 1, 2, 4)).reshape(B, L, H * HEAD_DIM)


def _last_chunk_bias(rel):
    t = np.arange(TQ)[:, None]
    j = np.arange(TK)[None, :]
    last = _bias_table(rel, t + TK - TQ - j)
    far = jnp.broadcast_to(rel[N_BUCKETS - 1][:, None, None], (rel.shape[1], TQ, LANE))
    return jnp.concatenate([last, far], axis=2)


def dsa_prompt(q_d, qi, wi, ki, k_d, v_d, rel):
    B, L, _ = q_d.shape
    assert L % TK == 0 and DSA_KV * HEAD_DIM == LANE
    topk = min(DSA_TOPK_MAX, L // 4)
    nch = L // TK
    front = ((0, 0), (L - TQ, 0), (0, 0))
    qp = _pad_heads(q_d * HEAD_DIM ** -0.5, DSA_KV).astype(MXU_DTYPE)
    ki2 = jnp.pad(jnp.concatenate([ki, ki], axis=-1), front).astype(MXU_DTYPE)
    kp = jnp.pad(k_d, front).astype(MXU_DTYPE)
    vp = jnp.pad(v_d, front).astype(MXU_DTYPE)
    bias = _last_chunk_bias(rel)
    rows = 2 * L - TQ
    out = pl.pallas_call(
        partial(_dsa_body, L=L, topk=topk),
        grid=(B, L // TQ),
        in_specs=[pl.BlockSpec((1, TQ, IDX_HEADS * IDX_DIM), lambda b, i: (b, i, 0)),
                  pl.BlockSpec((1, TQ, IDX_HEADS), lambda b, i: (b, i, 0)),
                  pl.BlockSpec((1, rows, LANE), lambda b, i: (b, 0, 0)),
                  pl.BlockSpec((1, TQ, DSA_HEADS * LANE), lambda b, i: (b, i, 0)),
                  pl.BlockSpec((1, rows, LANE), lambda b, i: (b, 0, 0)),
                  pl.BlockSpec((1, rows, LANE), lambda b, i: (b, 0, 0)),
                  pl.BlockSpec((DSA_HEADS, TQ, TK + LANE), lambda b, i: (0, 0, 0))],
        out_specs=pl.BlockSpec((1, DSA_HEADS, TQ, LANE), lambda b, i: (b, 0, i, 0)),
        out_shape=jax.ShapeDtypeStruct((B, DSA_HEADS, L, LANE), jnp.float32),
        scratch_shapes=[pltpu.VMEM((nch, TQ, TK), jnp.int32), pltpu.VMEM((nch, TQ, TK), jnp.float32)],
        compiler_params=pltpu.CompilerParams(dimension_semantics=("arbitrary", "arbitrary"),
                                             vmem_limit_bytes=VMEM_LIMIT),
        name="dsa_prompt",
    )(qi.astype(MXU_DTYPE) * IDX_DIM ** -0.5, wi, ki2, qp, kp, vp, bias)
    return _unpad_heads(out, DSA_KV)


def nsa_prompt(q_n, gate, kc, vc, k_s, v_s, k_w, v_w, rel):
    B, L, _ = q_n.shape
    assert L % TK == 0 and L >= WINDOW + TQ and NSA_KV * HEAD_DIM == LANE
    nch = L // TK
    NS = L // SEL_BLOCK
    n_sel = min(N_SEL, NS)
    NC = kc.shape[1]
    cq = TQ // CMP_STRIDE
    WC = max(-(-(L // CMP_STRIDE) // LANE) * LANE, LANE)
    rows_c = WC - cq + L // CMP_STRIDE
    qp = _pad_heads(q_n * HEAD_DIM ** -0.5, NSA_KV).astype(MXU_DTYPE)
    pad_c = ((0, 0), (WC - cq, rows_c - (WC - cq) - NC), (0, 0))
    kcp, vcp = jnp.pad(kc, pad_c), jnp.pad(vc, pad_c)
    front = ((0, 0), (L - TQ, 0), (0, 0))
    ksp, vsp = jnp.pad(k_s, front).astype(MXU_DTYPE), jnp.pad(v_s, front).astype(MXU_DTYPE)
    wfront = ((0, 0), (WINDOW, 0), (0, 0))
    kwp, vwp = jnp.pad(k_w, wfront).astype(MXU_DTYPE), jnp.pad(v_w, wfront).astype(MXU_DTYPE)
    t = np.arange(TQ)[:, None]
    cb = _bias_table(rel, t - CMP_STRIDE * (np.arange(WC)[None, :] - (WC - cq)) - (CMP_LEN - 1))
    c0 = (np.arange(WC) - (WC - cq)) * CMP_STRIDE
    s0 = (np.arange(NS) - (NS - TQ // SEL_BLOCK)) * SEL_BLOCK
    ov = np.minimum(c0[:, None] + CMP_LEN, s0[None, :] + SEL_BLOCK) - np.maximum(c0[:, None], s0[None, :])
    mfix = jnp.asarray(np.clip(ov, 0, None) / CMP_LEN, jnp.bfloat16)
    e3 = (np.arange(L)[None, :] // SEL_BLOCK == np.arange(NS)[:, None]).reshape(NS, nch, TK).transpose(1, 0, 2)
    e3 = jnp.asarray(e3, jnp.bfloat16)
    sbias = _last_chunk_bias(rel)
    WW = WINDOW + TQ
    wbias = _bias_table(rel, t + WINDOW - np.arange(WW)[None, :])
    rows = 2 * L - TQ
    full = lambda shp: pl.BlockSpec(shp, lambda b, i: (0,) * len(shp))
    per_b = lambda r: pl.BlockSpec((1, r, LANE), lambda b, i: (b, 0, 0))
    out = pl.pallas_call(
        partial(_nsa_body, L=L, WC=WC, NS=NS, n_sel=n_sel),
        grid=(B, L // TQ),
        in_specs=[pl.BlockSpec((1, TQ, NSA_HEADS * LANE), lambda b, i: (b, i, 0)),
                  pl.BlockSpec((1, TQ, 3 * NSA_HEADS), lambda b, i: (b, i, 0)),
                  per_b(rows_c), per_b(rows_c), per_b(rows), per_b(rows), per_b(L + WINDOW), per_b(L + WINDOW),
                  full((NSA_HEADS, TQ, WC)), full((WC, NS)), full((nch, NS, TK)),
                  full((NSA_HEADS, TQ, TK + LANE)), full((NSA_HEADS, TQ, WW))],
        out_specs=pl.BlockSpec((1, NSA_HEADS, TQ, LANE), lambda b, i: (b, 0, i, 0)),
        out_shape=jax.ShapeDtypeStruct((B, NSA_HEADS, L, LANE), jnp.float32),
        scratch_shapes=[pltpu.VMEM((nch, TQ, TK), jnp.float32)],
        compiler_params=pltpu.CompilerParams(dimension_semantics=("arbitrary", "arbitrary"),
                                             vmem_limit_bytes=VMEM_LIMIT),
        name="nsa_prompt",
    )(qp, gate, kcp, vcp, ksp, vsp, kwp, vwp, cb, mfix, e3, sbias, wbias)
    return _unpad_heads(out, NSA_KV)


TM_R = 256
TM_E = 256
TM_C = 128
DMA_UNROLL = 8


def _router_body(x_ref, g_ref, wr_ref, br_ref, h_ref, meta_ref, cnt_ref, carry_scr):
    i = pl.program_id(0)

    @pl.when(i == 0)
    def _():
        carry_scr[...] = jnp.zeros_like(carry_scr)

    x = x_ref[...]
    h = x * lax.rsqrt(jnp.mean(x * x, axis=-1, keepdims=True) + RMS_EPS) * g_ref[...]
    h_ref[...] = h
    lane = lax.broadcasted_iota(jnp.int32, (TM_R, LANE), 1)
    lane_f = lane.astype(jnp.float32)
    logits = jnp.dot(h.astype(MXU_DTYPE), wr_ref[...], preferred_element_type=jnp.float32) + br_ref[...]
    cur = jnp.where(lane < N_EXPERTS, logits, -jnp.inf)
    vals, idxs = [], []
    for _ in range(TOP_K):
        m = jnp.max(cur, axis=1, keepdims=True)
        idx = jnp.min(jnp.where(cur == m, lane_f, float(LANE)), axis=1, keepdims=True)
        vals.append(m)
        idxs.append(idx)
        cur = jnp.where(lane_f == idx, -jnp.inf, cur)
    es = [jnp.exp(v - vals[0]) for v in vals]
    denom = es[0]
    for e in es[1:]:
        denom = denom + e
    onehots = [jnp.where(lane_f == idx, 1.0, 0.0) for idx in idxs]
    osum = onehots[0]
    for o in onehots[1:]:
        osum = osum + o
    r_io = lax.broadcasted_iota(jnp.int32, (TM_R, TM_R), 0)
    c_io = lax.broadcasted_iota(jnp.int32, (TM_R, TM_R), 1)
    ltri = jnp.where(c_io < r_io, 1.0, 0.0).astype(jnp.bfloat16)
    prefix = jnp.dot(ltri, osum.astype(jnp.bfloat16), preferred_element_type=jnp.float32) + carry_scr[...]
    meta = jnp.zeros((TM_R, LANE), jnp.float32)
    for k in range(TOP_K):
        pos = jnp.sum(onehots[k] * prefix, axis=1, keepdims=True)
        meta = jnp.where(lane == k, idxs[k], meta)
        meta = jnp.where(lane == TOP_K + k, pos, meta)
        meta = jnp.where(lane == 2 * TOP_K + k, es[k] / denom, meta)
    meta_ref[...] = meta
    carry_scr[...] = carry_scr[...] + jnp.sum(osum, axis=0, keepdims=True)
    cnt_ref[...] = carry_scr[...]


def _scatter_body(dest_ref, h_ref, xs_in_ref, xs_ref, sem):
    del xs_in_ref
    n = TM_R * TOP_K

    def row_copy(j):
        return pltpu.make_async_copy(h_ref.at[pl.ds(j // TOP_K, 1), :], xs_ref.at[pl.ds(dest_ref[j], 1), :], sem)

    def issue(j, c):
        row_copy(j).start()
        return c

    def drain(j, c):
        row_copy(j).wait()
        return c

    lax.fori_loop(0, n, issue, 0, unroll=DMA_UNROLL)
    lax.fori_loop(0, n, drain, 0, unroll=DMA_UNROLL)


def _expert_body(blk_e_ref, nblk_ref, xs_ref, wgu_ref, bgu_ref, wd_ref, bd_ref, ys_ref):
    del blk_e_ref
    i = pl.program_id(0)

    @pl.when(i < nblk_ref[0])
    def _():
        gu = jnp.dot(xs_ref[...].astype(MXU_DTYPE), wgu_ref[0], preferred_element_type=jnp.float32) + bgu_ref[0]
        g = jnp.minimum(gu[:, :D_FF], SWIGLU_LIMIT)
        u = jnp.clip(gu[:, D_FF:], -SWIGLU_LIMIT, SWIGLU_LIMIT)
        hb = (u + 1.0) * (g * jax.nn.sigmoid(SWIGLU_ALPHA * g))
        ys_ref[...] = jnp.dot(hb.astype(MXU_DTYPE), wd_ref[0], preferred_element_type=jnp.float32) + bd_ref[0]

    @pl.when(i >= nblk_ref[0])
    def _():
        ys_ref[...] = jnp.zeros_like(ys_ref)


def _combine_body(dest_ref, meta_ref, x_ref, g_ref, ys_ref, y_ref, buf, sem):
    n = TM_C * TOP_K

    def row_copy(j):
        return pltpu.make_async_copy(ys_ref.at[pl.ds(dest_ref[j], 1), :],
                                     buf.at[j % TOP_K, pl.ds(j // TOP_K, 1), :], sem)

    def issue(j, c):
        row_copy(j).start()
        return c

    def drain(j, c):
        row_copy(j).wait()
        return c

    lax.fori_loop(0, n, issue, 0, unroll=DMA_UNROLL)
    lax.fori_loop(0, n, drain, 0, unroll=DMA_UNROLL)
    acc = x_ref[...]
    meta = meta_ref[...]
    for k in range(TOP_K):
        acc = acc + meta[:, 2 * TOP_K + k:2 * TOP_K + k + 1] * buf[k]
    y_ref[...] = acc * lax.rsqrt(jnp.mean(acc * acc, axis=-1, keepdims=True) + RMS_EPS) * g_ref[...]


def moe_block(x, ln_ffn, w_router, b_router, w_gu, b_gu, w_down, b_down, ln_f):
    n, d = x.shape
    assert n % TM_R == 0 and n % TM_C == 0 and N_EXPERTS <= LANE
    cparams = pltpu.CompilerParams(dimension_semantics=("arbitrary",), vmem_limit_bytes=VMEM_LIMIT)
    wr = jnp.pad(w_router, ((0, 0), (0, LANE - N_EXPERTS))).astype(MXU_DTYPE)
    br = jnp.pad(b_router, (0, LANE - N_EXPERTS)).reshape(1, LANE)
    h, meta, cnt = pl.pallas_call(
        _router_body,
        grid=(n // TM_R,),
        in_specs=[pl.BlockSpec((TM_R, d), lambda i: (i, 0)), pl.BlockSpec((1, d), lambda i: (0, 0)),
                  pl.BlockSpec((d, LANE), lambda i: (0, 0)), pl.BlockSpec((1, LANE), lambda i: (0, 0))],
        out_specs=[pl.BlockSpec((TM_R, d), lambda i: (i, 0)), pl.BlockSpec((TM_R, LANE), lambda i: (i, 0)),
                   pl.BlockSpec((1, LANE), lambda i: (0, 0))],
        out_shape=[jax.ShapeDtypeStruct((n, d), jnp.float32), jax.ShapeDtypeStruct((n, LANE), jnp.float32),
                   jax.ShapeDtypeStruct((1, LANE), jnp.float32)],
        scratch_shapes=[pltpu.VMEM((1, LANE), jnp.float32)],
        compiler_params=cparams, name="moe_router",
    )(x, ln_ffn.reshape(1, d), wr, br)

    top_e = meta[:, :TOP_K].astype(jnp.int32)
    pos = meta[:, TOP_K:2 * TOP_K].astype(jnp.int32)
    counts = cnt[0, :N_EXPERTS].astype(jnp.int32)
    padded = (counts + TM_E - 1) // TM_E * TM_E
    pad_end = jnp.cumsum(padded)
    dest = ((pad_end - padded)[top_e] + pos).reshape(-1)
    n_blocks = n * TOP_K // TM_E + N_EXPERTS
    blk_e = jnp.minimum(jnp.searchsorted(pad_end, jnp.arange(n_blocks) * TM_E, side='right'),
                        N_EXPERTS - 1).astype(jnp.int32)
    n_used = (pad_end[-1] // TM_E).reshape(1).astype(jnp.int32)

    xs = pl.pallas_call(
        _scatter_body,
        grid=(n // TM_R,),
        in_specs=[pl.BlockSpec((TM_R * TOP_K,), lambda i: (i,), memory_space=pltpu.SMEM),
                  pl.BlockSpec((TM_R, d), lambda i: (i, 0)),
                  pl.BlockSpec(memory_space=pl.ANY)],
        out_specs=pl.BlockSpec(memory_space=pl.ANY),
        out_shape=jax.ShapeDtypeStruct((n_blocks * TM_E, d), jnp.float32),
        scratch_shapes=[pltpu.SemaphoreType.DMA(())],
        input_output_aliases={2: 0},
        compiler_params=cparams, name="moe_scatter",
    )(dest, h, jnp.zeros((n_blocks * TM_E, d), jnp.float32))

    ys = pl.pallas_call(
        _expert_body,
        grid_spec=pltpu.PrefetchScalarGridSpec(
            num_scalar_prefetch=2,
            grid=(n_blocks,),
            in_specs=[pl.BlockSpec((TM_E, d), lambda i, be, nb: (i, 0)),
                      pl.BlockSpec((1, d, 2 * D_FF), lambda i, be, nb: (be[i], 0, 0)),
                      pl.BlockSpec((1, 1, 2 * D_FF), lambda i, be, nb: (be[i], 0, 0)),
                      pl.BlockSpec((1, D_FF, d), lambda i, be, nb: (be[i], 0, 0)),
                      pl.BlockSpec((1, 1, d), lambda i, be, nb: (be[i], 0, 0))],
            out_specs=pl.BlockSpec((TM_E, d), lambda i, be, nb: (i, 0))),
        out_shape=jax.ShapeDtypeStruct((n_blocks * TM_E, d), jnp.float32),
        compiler_params=cparams, name="moe_experts",
    )(blk_e, n_used, xs, w_gu.astype(MXU_DTYPE), b_gu.reshape(N_EXPERTS, 1, 2 * D_FF),
      w_down.astype(MXU_DTYPE), b_down.reshape(N_EXPERTS, 1, d))

    return pl.pallas_call(
        _combine_body,
        grid=(n // TM_C,),
        in_specs=[pl.BlockSpec((TM_C * TOP_K,), lambda i: (i,), memory_space=pltpu.SMEM),
                  pl.BlockSpec((TM_C, LANE), lambda i: (i, 0)),
                  pl.BlockSpec((TM_C, d), lambda i: (i, 0)),
                  pl.BlockSpec((1, d), lambda i: (0, 0)),
                  pl.BlockSpec(memory_space=pl.ANY)],
        out_specs=pl.BlockSpec((TM_C, d), lambda i: (i, 0)),
        out_shape=jax.ShapeDtypeStruct((n, d), jnp.float32),
        scratch_shapes=[pltpu.VMEM((TOP_K, TM_C, d), jnp.float32), pltpu.SemaphoreType.DMA(())],
        compiler_params=cparams, name="moe_combine",
    )(dest, meta, x, ln_f.reshape(1, d), ys)


PAGES_PER_STEP = TK // PAGE_SIZE
CMP_PAGES_PER_STEP = 8
ROWS_T = 8
ROWS_H = 16


def _page_specs(feat, n_pages, per_step=PAGES_PER_STEP):
    def spec(i):
        return pl.BlockSpec((1, feat, PAGE_SIZE),
                            lambda b, P, pt: (pt[b, jnp.minimum(P * per_step + i, n_pages - 1)], 0, 0))
    return [spec(i) for i in range(per_step)]


def _chunk(pages, new_ref, is_new):
    cache = jnp.concatenate([p[0] for p in pages], axis=1)
    return jnp.where(is_new, new_ref[0], cache)


def _idx_score_body(pt_ref, qi_ref, wi_ref, new_ref, *rest):
    del pt_ref
    pages, out_ref = rest[:PAGES_PER_STEP], rest[PAGES_PER_STEP]
    is_new = pl.program_id(1) == pl.num_programs(1) - 1
    kt = _chunk(pages, new_ref, is_new).astype(MXU_DTYPE)
    wi = wi_ref[0]
    sc = jnp.zeros((ROWS_T, TK), jnp.float32)
    for h in range(IDX_HEADS):
        sc = sc + jnp.maximum(jnp.dot(qi_ref[0, h], kt, preferred_element_type=jnp.float32), 0.0) * wi[:, h:h + 1]
    out_ref[0] = sc


def _topk_rows_body(sc_ref, out_ref, key_scr, *, nch, topk, past_len):
    t_io = lax.broadcasted_iota(jnp.int32, (TQ, TK), 0)
    j_io = lax.broadcasted_iota(jnp.int32, (TQ, TK), 1)
    qpos = past_len + t_io % ROWS_T

    def valid_chunk(c):
        return j_io + c * TK <= qpos

    def load(c, carry):
        key_scr[c] = jnp.where(valid_chunk(c), _order_key(sc_ref[c]), INT_MIN)
        return carry

    lax.fori_loop(0, nch, load, 0)
    thr2 = _select_topk(key_scr, 0, nch, topk, nch * TK)

    def finalize(c, carry):
        out_ref[c] = jnp.where(valid_chunk(c) & (key_scr[c] >= thr2), 1.0, 0.0)
        return carry

    lax.fori_loop(0, nch, finalize, 0)


def _decode_attn_body(pt_ref, q_ref, selblk_ref, e3_ref, dmask_ref, bias_ref, snew_ref, dnew_ref, *rest,
                      past_len):
    del pt_ref
    n = PAGES_PER_STEP
    spages, dpages, o_ref, m_scr, l_scr, acc_scr = rest[:n], rest[n:2 * n], rest[2 * n], *rest[2 * n + 1:]
    P = pl.program_id(1)
    last = pl.num_programs(1) - 1
    is_new = P == last

    @pl.when(P == 0)
    def _():
        m_scr[...] = jnp.full_like(m_scr, M_INIT)
        l_scr[...] = jnp.zeros_like(l_scr)
        acc_scr[...] = jnp.zeros_like(acc_scr)

    rows = NSA_KV * ROWS_H
    row = lax.broadcasted_iota(jnp.int32, (rows, TK), 0)
    kidx = lax.broadcasted_iota(jnp.int32, (rows, TK), 1) + P * TK
    causal = kidx <= past_len + row % (ROWS_H // NSA_REP)
    rep = jnp.where(lax.broadcasted_iota(jnp.int32, (rows, ROWS_T), 0) % (ROWS_H // DSA_REP)
                    == lax.broadcasted_iota(jnp.int32, (rows, ROWS_T), 1), 1.0, 0.0).astype(jnp.bfloat16)
    dm = jnp.dot(rep, dmask_ref[0].astype(jnp.bfloat16), preferred_element_type=jnp.float32)
    hit = jnp.dot(selblk_ref[0], e3_ref[0], preferred_element_type=jnp.float32)
    valids = ((hit > 0.5) & causal, dm > 0.5)
    bsel = jnp.where(is_new, 2, jnp.where(P == last - 1, 1, 0))
    for br, (pages, new_ref) in enumerate(((spages, snew_ref), (dpages, dnew_ref))):
        data = _chunk(pages, new_ref, is_new)
        kt = data[:LANE].astype(MXU_DTYPE)
        vt = data[LANE:].astype(MXU_DTYPE)
        s = (jnp.dot(q_ref[0, br], kt, preferred_element_type=jnp.float32) + bias_ref[br, bsel]
             + jnp.where(valids[br], 0.0, NEG_INF))
        m_new = jnp.maximum(m_scr[br], jnp.max(s, axis=1, keepdims=True))
        alpha = jnp.exp(m_scr[br] - m_new)
        p = jnp.exp(s - m_new)
        l_scr[br] = alpha * l_scr[br] + jnp.sum(p, axis=1, keepdims=True)
        acc_scr[br] = alpha * acc_scr[br] + _dot_nt(p.astype(MXU_DTYPE), vt)
        m_scr[br] = m_new

    @pl.when(is_new)
    def _():
        for br in range(2):
            o_ref[0, br] = acc_scr[br] / l_scr[br]


def _token_minor(cache):
    n = cache.shape[0]
    return jnp.moveaxis(cache, 1, -1).reshape(n, -1, cache.shape[1])


def _new_chunk(rows):
    db, s = rows.shape[:2]
    t = jnp.moveaxis(rows.reshape(db, s, -1), 1, 2)
    return jnp.pad(t, ((0, 0), (0, 0), (0, TK - s)))


def _rows_rt(x, n_kv):
    db, s, hd = x.shape
    rep = hd // HEAD_DIM // n_kv
    xp = _pad_heads(x, n_kv).reshape(db, s, n_kv, rep, LANE)
    return jnp.transpose(xp, (0, 2, 3, 1, 4)).reshape(db, n_kv, rep * s, LANE)


def _sample_compress_body(pt_ref, wbd_ref, pe_ref, w2_ref, *rest):
    del pt_ref
    pages, o_ref, xs_scr, acc_scr = rest[:CMP_PAGES_PER_STEP], *rest[CMP_PAGES_PER_STEP:]
    P = pl.program_id(1)
    data = jnp.concatenate([p[0] for p in pages], axis=1)
    tok = CMP_PAGES_PER_STEP * PAGE_SIZE
    for f in range(xs_scr.shape[0]):
        xs_scr[f, pl.ds(pl.multiple_of(P * tok, tok), tok), :] = data[f * LANE:(f + 1) * LANE].T

    @pl.when(P == pl.num_programs(1) - 1)
    def _():
        n_chunks = xs_scr.shape[1] // CMP_STRIDE
        for s in range(CMP_STRIDE):
            rows = jnp.concatenate([xs_scr[f, pl.ds(s, n_chunks, stride=CMP_STRIDE), :]
                                    for f in range(xs_scr.shape[0])], axis=1)
            part = jnp.dot(rows.astype(MXU_DTYPE), wbd_ref[s], preferred_element_type=jnp.float32)
            if s == 0:
                acc_scr[...] = part
            else:
                acc_scr[...] += part
        half = acc_scr.shape[1] // 2
        hid = acc_scr[:, :half] + pltpu.roll(acc_scr[:, half:], n_chunks - 1, 0) + pe_ref[...]
        act = hid * jax.nn.sigmoid(hid)
        o_ref[0] = jnp.dot(act.astype(MXU_DTYPE), w2_ref[...], preferred_element_type=jnp.float32)


def sample_compress(cache_cmp, page_table, k_pe, k_w1, k_w2, v_pe, v_w1, v_w2):
    DB, n_pages = page_table.shape
    assert CMP_LEN == 2 * CMP_STRIDE and n_pages % CMP_PAGES_PER_STEP == 0
    past_len = n_pages * PAGE_SIZE
    n_chunks = past_len // CMP_STRIDE
    combos = 2 * NSA_KV
    feat = combos * HEAD_DIM
    w1 = jnp.stack([k_w1] * NSA_KV + [v_w1] * NSA_KV).reshape(combos, 2, CMP_STRIDE, HEAD_DIM, CMP_HIDDEN)
    wbd = jnp.einsum('cisdh,xc->sxdich', w1, jnp.eye(combos, dtype=w1.dtype))
    wbd = wbd.reshape(CMP_STRIDE, feat, 2 * combos * CMP_HIDDEN).astype(MXU_DTYPE)
    pe = jnp.concatenate([jnp.einsum('ld,ldh->h', k_pe, k_w1)] * NSA_KV + [jnp.einsum('ld,ldh->h', v_pe, v_w1)] * NSA_KV)
    w2 = jnp.stack([k_w2] * NSA_KV + [v_w2] * NSA_KV)
    w2bd = jnp.einsum('chd,xc->chxd', w2, jnp.eye(combos, dtype=w2.dtype)).reshape(combos * CMP_HIDDEN, feat)
    return pl.pallas_call(
        _sample_compress_body,
        grid_spec=pltpu.PrefetchScalarGridSpec(
            num_scalar_prefetch=1, grid=(DB, n_pages // CMP_PAGES_PER_STEP),
            in_specs=[pl.BlockSpec((CMP_STRIDE, feat, 2 * combos * CMP_HIDDEN), lambda b, P, pt: (0, 0, 0)),
                      pl.BlockSpec((1, combos * CMP_HIDDEN), lambda b, P, pt: (0, 0)),
                      pl.BlockSpec((combos * CMP_HIDDEN, feat), lambda b, P, pt: (0, 0))]
            + _page_specs(feat, n_pages, CMP_PAGES_PER_STEP),
            out_specs=pl.BlockSpec((1, n_chunks, feat), lambda b, P, pt: (b, 0, 0)),
            scratch_shapes=[pltpu.VMEM((feat // LANE, past_len, LANE), jnp.float32),
                            pltpu.VMEM((n_chunks, 2 * combos * CMP_HIDDEN), jnp.float32)]),
        out_shape=jax.ShapeDtypeStruct((DB, n_chunks, feat), jnp.float32),
        compiler_params=pltpu.CompilerParams(dimension_semantics=("arbitrary", "arbitrary"),
                                             vmem_limit_bytes=VMEM_LIMIT),
        name="sample_compress",
    )(page_table, wbd, pe.reshape(1, -1), w2bd.astype(MXU_DTYPE), *([_token_minor(cache_cmp)] * CMP_PAGES_PER_STEP))


def sample_sparse_attention(q_n, q_d, qi, wi, ki_new, selblk, sel_new, dsa_new, cache_sel, cache_dsa, cache_idx,
                            page_table, rel_bias):
    DB, S, _ = q_n.shape
    n_pages = page_table.shape[1]
    past_len = n_pages * PAGE_SIZE
    assert n_pages % PAGES_PER_STEP == 0 and S <= ROWS_T and NSA_REP * S == ROWS_H and DSA_REP * S == ROWS_H
    nch = n_pages // PAGES_PER_STEP + 1
    topk = min(DSA_TOPK_MAX, (past_len + S) // 4)
    cparams = pltpu.CompilerParams(dimension_semantics=("arbitrary", "arbitrary"), vmem_limit_bytes=VMEM_LIMIT)

    qi8 = jnp.pad(jnp.transpose(qi.reshape(DB, S, IDX_HEADS, IDX_DIM), (0, 2, 1, 3)) * IDX_DIM ** -0.5,
                  ((0, 0), (0, 0), (0, ROWS_T - S), (0, 0))).astype(MXU_DTYPE)
    wi8 = jnp.pad(wi, ((0, 0), (0, ROWS_T - S), (0, 0)))
    scores = pl.pallas_call(
        _idx_score_body,
        grid_spec=pltpu.PrefetchScalarGridSpec(
            num_scalar_prefetch=1, grid=(DB, nch),
            in_specs=[pl.BlockSpec((1, IDX_HEADS, ROWS_T, IDX_DIM), lambda b, P, pt: (b, 0, 0, 0)),
                      pl.BlockSpec((1, ROWS_T, IDX_HEADS), lambda b, P, pt: (b, 0, 0)),
                      pl.BlockSpec((1, IDX_DIM, TK), lambda b, P, pt: (b, 0, 0))] + _page_specs(IDX_DIM, n_pages),
            out_specs=pl.BlockSpec((1, ROWS_T, TK), lambda b, P, pt: (P, b, 0))),
        out_shape=jax.ShapeDtypeStruct((nch, DB * ROWS_T, TK), jnp.float32),
        compiler_params=cparams, name="sample_idx_scores",
    )(page_table, qi8, wi8, _new_chunk(ki_new), *([_token_minor(cache_idx)] * PAGES_PER_STEP))

    rows = DB * ROWS_T
    assert rows % TQ == 0
    dmask = pl.pallas_call(
        partial(_topk_rows_body, nch=nch, topk=topk, past_len=past_len),
        grid=(rows // TQ,),
        in_specs=[pl.BlockSpec((nch, TQ, TK), lambda i: (0, i, 0))],
        out_specs=pl.BlockSpec((nch, TQ, TK), lambda i: (0, i, 0)),
        out_shape=jax.ShapeDtypeStruct((nch, rows, TK), jnp.float32),
        scratch_shapes=[pltpu.VMEM((nch, TQ, TK), jnp.int32)],
        compiler_params=pltpu.CompilerParams(dimension_semantics=("arbitrary",), vmem_limit_bytes=VMEM_LIMIT),
        name="sample_topk_mask",
    )(scores)

    rows = NSA_KV * ROWS_H
    q2 = jnp.stack([_rows_rt(q_n * HEAD_DIM ** -0.5, NSA_KV), _rows_rt(q_d * HEAD_DIM ** -0.5, DSA_KV)],
                   axis=1).reshape(DB, 2, rows, LANE).astype(MXU_DTYPE)
    nsb = selblk.shape[-1]
    nsb_pad = -(-(nch * TK // SEL_BLOCK) // LANE) * LANE
    sb = jnp.pad(jnp.transpose(selblk, (0, 2, 1, 3)), ((0, 0), (0, 0), (0, 0), (0, nsb_pad - nsb)))
    sb = jnp.tile(sb, (1, 1, NSA_REP, 1)).reshape(DB, rows, nsb_pad).astype(jnp.bfloat16)
    e3 = (np.arange(nch * TK)[None, :] // SEL_BLOCK == np.arange(nsb_pad)[:, None])
    e3 = jnp.asarray(e3.reshape(nsb_pad, nch, TK).transpose(1, 0, 2), jnp.bfloat16)
    t = np.tile(np.arange(S), ROWS_H // S)[:, None]
    j = np.arange(TK)[None, :]
    dist = np.stack([np.full((ROWS_H, TK), REL_MAX_DIST), t + TK - j, t - j])
    bias = _bias_table(rel_bias, dist)
    r_of_row = np.arange(ROWS_H) // S
    bias = jnp.stack([jnp.stack([jnp.concatenate([bias[br * NSA_HEADS + g * NSA_REP + r_of_row, v, np.arange(ROWS_H)]
                                                  for g in range(NSA_KV)]) for v in range(3)]) for br in range(2)])
    feat = 2 * NSA_KV * HEAD_DIM
    out = pl.pallas_call(
        partial(_decode_attn_body, past_len=past_len),
        grid_spec=pltpu.PrefetchScalarGridSpec(
            num_scalar_prefetch=1, grid=(DB, nch),
            in_specs=[pl.BlockSpec((1, 2, rows, LANE), lambda b, P, pt: (b, 0, 0, 0)),
                      pl.BlockSpec((1, rows, nsb_pad), lambda b, P, pt: (b, 0, 0)),
                      pl.BlockSpec((1, nsb_pad, TK), lambda b, P, pt: (P, 0, 0)),
                      pl.BlockSpec((1, ROWS_T, TK), lambda b, P, pt: (P, b, 0)),
                      pl.BlockSpec((2, 3, rows, TK), lambda b, P, pt: (0, 0, 0, 0)),
                      pl.BlockSpec((1, feat, TK), lambda b, P, pt: (b, 0, 0)),
                      pl.BlockSpec((1, feat, TK), lambda b, P, pt: (b, 0, 0))]
            + _page_specs(feat, n_pages) + _page_specs(feat, n_pages),
            out_specs=pl.BlockSpec((1, 2, rows, LANE), lambda b, P, pt: (b, 0, 0, 0)),
            scratch_shapes=[pltpu.VMEM((2, rows, 1), jnp.float32), pltpu.VMEM((2, rows, 1), jnp.float32),
                            pltpu.VMEM((2, rows, LANE), jnp.float32)]),
        out_shape=jax.ShapeDtypeStruct((DB, 2, rows, LANE), jnp.float32),
        compiler_params=cparams, name="sample_decode_attn",
    )(page_table, q2, sb, e3, dmask, bias, _new_chunk(sel_new), _new_chunk(dsa_new),
      *([_token_minor(cache_sel)] * PAGES_PER_STEP), *([_token_minor(cache_dsa)] * PAGES_PER_STEP))
    o = out.reshape(DB, 2, NSA_KV, NSA_REP, S, NSA_KV, HEAD_DIM)
    o = jnp.stack([o[:, :, g, :, :, g, :] for g in range(NSA_KV)], axis=2)
    o = jnp.transpose(o, (1, 0, 4, 2, 3, 5)).reshape(2, DB, S, NSA_HEADS * HEAD_DIM)
    return o[0], o[1]


TM_M = 512


def _merge_body(x_ref, on_ref, od_ref, gate_ref, wn_ref, wd_ref, wo_ref, o_ref):
    d = x_ref.shape[1]
    gate = jax.nn.sigmoid(gate_ref[...])
    a = jnp.dot(on_ref[...].astype(MXU_DTYPE), wn_ref[...], preferred_element_type=jnp.float32)
    b = jnp.dot(od_ref[...].astype(MXU_DTYPE), wd_ref[...], preferred_element_type=jnp.float32)
    mix = gate[:, :d] * a + gate[:, d:] * b
    o_ref[...] = x_ref[...] + jnp.dot(mix.astype(MXU_DTYPE), wo_ref[...], preferred_element_type=jnp.float32)


def merge_proj(x, o_n, o_d, merge_gate, w_nsa_out, w_dsa_out, w_o):
    n, d = x.shape
    tm = min(TM_M, n)
    assert n % tm == 0
    hd = o_n.shape[1]
    row = lambda c: pl.BlockSpec((tm, c), lambda i: (i, 0))
    full = lambda r, c: pl.BlockSpec((r, c), lambda i: (0, 0))
    return pl.pallas_call(
        _merge_body,
        grid=(n // tm,),
        in_specs=[row(d), row(hd), row(hd), row(2 * d), full(hd, d), full(hd, d), full(d, d)],
        out_specs=row(d),
        out_shape=jax.ShapeDtypeStruct((n, d), jnp.float32),
        compiler_params=pltpu.CompilerParams(dimension_semantics=("arbitrary",), vmem_limit_bytes=VMEM_LIMIT),
        name="merge_proj",
    )(x, o_n, o_d, merge_gate, w_nsa_out.astype(MXU_DTYPE), w_dsa_out.astype(MXU_DTYPE), w_o.astype(MXU_DTYPE))


def rmsnorm(x, g):
    xf = x.astype(jnp.float32)
    y = xf * lax.rsqrt(jnp.mean(xf * xf, axis=-1, keepdims=True) + RMS_EPS)
    return (y * g.astype(jnp.float32)).astype(x.dtype)


def split_proj(h):
    out, off = {}, 0
    for name, n in SPLITS:
        out[name] = h[..., off:off + n]
        off += n
    return out


def t5_bucket(dist):
    n = jnp.maximum(dist, 0)
    max_exact = N_BUCKETS // 2
    nf = jnp.maximum(n, 1).astype(jnp.float32)
    large = max_exact + (jnp.log(nf / max_exact) / math.log(REL_MAX_DIST / max_exact)
                         * (N_BUCKETS - max_exact)).astype(jnp.int32)
    return jnp.where(n < max_exact, n, jnp.minimum(large, N_BUCKETS - 1))


def masked_softmax(s, mask):
    p = jax.nn.softmax(jnp.where(mask, s, NEG_INF), axis=-1)
    return jnp.where(mask, p, 0.0)


def cmp_to_sel_matrix(nc, ns):
    c0 = np.arange(nc) * CMP_STRIDE
    s0 = np.arange(ns) * SEL_BLOCK
    ov = np.minimum(c0[:, None] + CMP_LEN, s0[None, :] + SEL_BLOCK) - np.maximum(c0[:, None], s0[None, :])
    return jnp.asarray(np.clip(ov, 0, None) / CMP_LEN, dtype=jnp.float32)


def compress(rows, pe, w1, w2):
    B, L, G, D = rows.shape
    r = CMP_LEN // CMP_STRIDE
    nch = L // CMP_STRIDE
    nc = nch - r + 1
    ch = rows[:, :nch * CMP_STRIDE].reshape(B, nch, CMP_STRIDE, G, D)
    w1c = w1.reshape(r, CMP_STRIDE, D, CMP_HIDDEN)
    hid = jnp.einsum('ld,ldh->h', pe, w1)
    for i in range(r):
        hid = hid + jnp.einsum('bnsgd,sdh->bngh', ch[:, i:i + nc], w1c[i])
    return jax.nn.silu(hid) @ w2


def nsa_dense_branches(q, q_pos, gates, kc, vc, c_end, n_keys, win_rows, w_pos, rel_tab):
    B, T = q.shape[:2]
    scale = HEAD_DIM ** -0.5
    qg = q.reshape(B, T, NSA_KV, NSA_REP, HEAD_DIM)
    d_c = q_pos[:, None] - c_end[None, :]
    s = jnp.einsum('btgrd,bcgd->btgrc', qg, kc).astype(jnp.float32) * scale
    s = s + jnp.moveaxis(rel_tab[t5_bucket(d_c)], 1, -1)
    p_c = masked_softmax(s, (d_c >= 0)[:, None, None, :])
    o_c = jnp.einsum('btgrc,bcgd->btgrd', p_c.astype(vc.dtype), vc)
    nc = kc.shape[1]
    ns = -(-n_keys // SEL_BLOCK)
    blk = jnp.einsum('btgrc,cj->btgj', p_c, cmp_to_sel_matrix(nc, ns))
    j = jnp.arange(ns)
    cur = (q_pos // SEL_BLOCK)[:, None]
    visible = ((j * SEL_BLOCK)[None, :] <= q_pos[:, None])[:, None, :]
    forced = ((j[None, :] == 0) | (j[None, :] == cur) | (j[None, :] == cur - 1))[:, None, :]
    score = jnp.where(visible, jnp.where(forced, FORCED_SCORE, blk), -jnp.inf)
    _, idx = lax.top_k(score, min(N_SEL, ns))
    d_w = q_pos[:, None] - w_pos[None, :]
    s = jnp.einsum('btgrd,bwgd->btgrw', qg, win_rows[:, :, 0]).astype(jnp.float32) * scale
    s = s + jnp.moveaxis(rel_tab[t5_bucket(d_w)], 1, -1)
    ok_w = (d_w >= 0) & (d_w <= WINDOW) & (w_pos >= 0)[None, :]
    p_w = masked_softmax(s, ok_w[:, None, None, :])
    o_w = jnp.einsum('btgrw,bwgd->btgrd', p_w.astype(win_rows.dtype), win_rows[:, :, 1])
    g = gates.reshape(B, T, NSA_KV, NSA_REP, 3).astype(o_c.dtype)
    return o_c, o_w, g, idx


def mem_attend(h, mem_kv, w_q, w_o):
    B, T = h.shape[:2]
    q = (h @ w_q).reshape(B, T, X_HEADS, X_HEAD_DIM)
    s = jnp.einsum('bthd,bmhd->bthm', q, mem_kv[:, :, 0]).astype(jnp.float32) * X_HEAD_DIM ** -0.5
    p = jax.nn.softmax(s, axis=-1)
    o = jnp.einsum('bthm,bmhd->bthd', p.astype(mem_kv.dtype), mem_kv[:, :, 1])
    return o.reshape(B, T, X_HEADS * X_HEAD_DIM) @ w_o


def layer_tail(x, o_n, o_d, merge_gate, mem_kv, w_nsa_out, w_dsa_out, w_o, ln_mem, w_mem_q, w_mem_o):
    shp = x.shape
    flat = lambda a: a.reshape(-1, a.shape[-1])
    x = merge_proj(flat(x), flat(o_n), flat(o_d), flat(merge_gate), w_nsa_out, w_dsa_out, w_o).reshape(shp)
    return x + mem_attend(rmsnorm(x, ln_mem), mem_kv, w_mem_q, w_mem_o)


def kernel(x_prompt, x_sample, cache_nsa_cmp_kv, cache_nsa_sel_kv, cache_nsa_win_kv, cache_dsa_kv,
           cache_dsa_idx_k, cache_mem_kv, page_table, mem_prompt, ln_mix, w_in, cmp_k_pe, cmp_k_w1, cmp_k_w2,
           cmp_v_pe, cmp_v_w1, cmp_v_w2, rel_bias, w_nsa_out, w_dsa_out, w_o, ln_mem, w_mem_q, w_mem_kv,
           w_mem_o, ln_ffn, w_router, b_router, w_gu, b_gu, w_down, b_down, ln_f):
    rel_nsa = rel_bias[:, :NSA_HEADS].reshape(N_BUCKETS, NSA_KV, NSA_REP)
    rel_dsa = rel_bias[:, NSA_HEADS:].reshape(N_BUCKETS, DSA_KV, DSA_REP)
    tail_w = (w_nsa_out, w_dsa_out, w_o, ln_mem, w_mem_q, w_mem_o)

    B, L = x_prompt.shape[:2]
    pp = split_proj(norm_proj(x_prompt.reshape(B * L, D_MODEL), ln_mix, w_in).reshape(B, L, D_IN))
    kv6 = pp["nsa_kv"].reshape(B, L, 6, NSA_KV, HEAD_DIM)
    kc = compress(kv6[:, :, 0], cmp_k_pe, cmp_k_w1, cmp_k_w2)
    vc = compress(kv6[:, :, 1], cmp_v_pe, cmp_v_w1, cmp_v_w2)
    k6 = pp["nsa_kv"].reshape(B, L, 6, NSA_KV * HEAD_DIM)
    NC = kc.shape[1]
    o_n = nsa_prompt(pp["nsa_q"], pp["nsa_gate"], kc.reshape(B, NC, NSA_KV * HEAD_DIM),
                     vc.reshape(B, NC, NSA_KV * HEAD_DIM), k6[:, :, 2], k6[:, :, 3], k6[:, :, 4], k6[:, :, 5],
                     rel_bias[:, :NSA_HEADS])
    dsa_rows = pp["dsa_kv"].reshape(B, L, 2, DSA_KV, HEAD_DIM)
    dk = pp["dsa_kv"].reshape(B, L, 2, DSA_KV * HEAD_DIM)
    ki = pp["idx_k"]
    o_d = dsa_prompt(pp["dsa_q"], pp["idx_q"], pp["idx_w"], ki, dk[:, :, 0], dk[:, :, 1], rel_bias[:, NSA_HEADS:])
    mem_kv_p = (mem_prompt @ w_mem_kv).reshape(B, MEM_LEN, 2, X_HEADS, X_HEAD_DIM)
    x2_prompt = layer_tail(x_prompt, o_n, o_d, pp["merge_gate"], mem_kv_p, *tail_w)

    DB, S = x_sample.shape[:2]
    past_len = page_table.shape[1] * cache_dsa_kv.shape[1]
    n_keys = past_len + S
    q_pos = past_len + jnp.arange(S)
    ps = split_proj(norm_proj(x_sample.reshape(DB * S, D_MODEL), ln_mix, w_in).reshape(DB, S, D_IN))
    q_ns = ps["nsa_q"].reshape(DB, S, NSA_HEADS, HEAD_DIM)
    g_ns = jax.nn.sigmoid(ps["nsa_gate"]).reshape(DB, S, NSA_HEADS, 3)
    kv6s = ps["nsa_kv"].reshape(DB, S, 6, NSA_KV, HEAD_DIM)
    assert (past_len + S) // CMP_STRIDE == past_len // CMP_STRIDE
    n_cmp = past_len // CMP_STRIDE - CMP_LEN // CMP_STRIDE + 1
    kvc = sample_compress(cache_nsa_cmp_kv, page_table, cmp_k_pe, cmp_k_w1, cmp_k_w2, cmp_v_pe, cmp_v_w1, cmp_v_w2)
    kvc = kvc[:, :n_cmp].reshape(DB, n_cmp, 2, NSA_KV, HEAD_DIM)
    kc_s, vc_s = kvc[:, :, 0], kvc[:, :, 1]
    c_end_s = jnp.arange(kc_s.shape[1]) * CMP_STRIDE + CMP_LEN - 1
    win_rows_s = jnp.concatenate([cache_nsa_win_kv, kv6s[:, :, 4:6]], axis=1)
    wb = cache_nsa_win_kv.shape[1]
    w_pos_s = past_len - wb + jnp.arange(wb + S)
    o_cs, o_ws, g_s, sel_idx = nsa_dense_branches(q_ns, q_pos, g_ns, kc_s, vc_s, c_end_s, n_keys, win_rows_s,
                                                  w_pos_s, rel_nsa)
    n_blk = -(-n_keys // SEL_BLOCK)
    selblk = jnp.sum(jax.nn.one_hot(sel_idx, n_blk, dtype=jnp.float32), axis=-2)
    dsa_rows_s = ps["dsa_kv"].reshape(DB, S, 2, DSA_KV, HEAD_DIM)
    o_ss, o_ds = sample_sparse_attention(ps["nsa_q"], ps["dsa_q"], ps["idx_q"], ps["idx_w"], ps["idx_k"], selblk,
                                         kv6s[:, :, 2:4], dsa_rows_s, cache_nsa_sel_kv, cache_dsa_kv,
                                         cache_dsa_idx_k, page_table, rel_bias)
    o_ns = (o_cs * g_s[..., 0:1] + o_ss.reshape(o_cs.shape) * g_s[..., 1:2] + o_ws * g_s[..., 2:3])
    o_ns = o_ns.reshape(DB, S, NSA_HEADS * HEAD_DIM)
    x2_sample = layer_tail(x_sample, o_ns, o_ds, ps["merge_gate"], cache_mem_kv, *tail_w)

    x2 = jnp.concatenate([x2_prompt.reshape(B * L, D_MODEL), x2_sample.reshape(DB * S, D_MODEL)], axis=0)
    y = moe_block(x2, ln_ffn, w_router, b_router, w_gu, b_gu, w_down, b_down, ln_f)
    y_prompt = y[:B * L].reshape(B, L, D_MODEL)
    y_sample = y[B * L:].reshape(DB, S, D_MODEL)
    L_w = min(WINDOW, L)
    return (y_prompt, y_sample,
            kv6[:, :, 0:2], kv6[:, :, 2:4], kv6[:, L - L_w:, 4:6], dsa_rows, ki, mem_kv_p,
            kv6s[:, :, 0:2], kv6s[:, :, 2:4], win_rows_s[:, -wb:], dsa_rows_s, ps["idx_k"])
```
